```python
import numpy as np
import jax
import jax.numpy as jnp
from jax import lax

D_MODEL = 2048
BATCH = 4
SEQ = 4096
DEPTH = 4

CHUNK = 64
LEFT_CHUNKS = 8
BAND = LEFT_CHUNKS + 1
N_HEADS = 8
HEAD_DIM = 128
ATT_WIDTH = N_HEADS * HEAD_DIM
MAX_REL_DIST = 256
N_REL = CHUNK + MAX_REL_DIST
LRU_WIDTH = 1024
LRU_BLOCKS = 8
LRU_BLOCK_W = LRU_WIDTH // LRU_BLOCKS
CONV_W = 4
LRU_C = 8.0
D_FF = 6144
N_EXPERTS = 8
TOP_K = 2
D_FF_EXPERT = 7168
EXPERT_BLOCK = 256
N_DENSE = (DEPTH + 1) // 2
N_MOE = DEPTH // 2
RMS_EPS = 1e-6
NEG_INF = -1e30
IN_WIDTH = 3 * ATT_WIDTH + 2 * LRU_WIDTH + 2 * D_MODEL
SPLIT_POINTS = (ATT_WIDTH, 2 * ATT_WIDTH, 3 * ATT_WIDTH,
                3 * ATT_WIDTH + LRU_WIDTH, 3 * ATT_WIDTH + 2 * LRU_WIDTH,
                3 * ATT_WIDTH + 2 * LRU_WIDTH + D_MODEL)

kernel_name = "hybrid_chunked_attn_rglru_moe_trunk"


def rms_norm(x, g):
    xf = x.astype(jnp.float32)
    y = xf * lax.rsqrt(jnp.mean(xf * xf, axis=-1, keepdims=True) + RMS_EPS)
    return (y * g.astype(jnp.float32)).astype(x.dtype)


def rel_index():
    qi = np.arange(CHUNK)[:, None, None]
    j = np.arange(BAND)[None, :, None]
    ki = np.arange(CHUNK)[None, None, :]
    dist = (LEFT_CHUNKS - j) * CHUNK + qi - ki
    return (np.clip(dist, -(CHUNK - 1), MAX_REL_DIST) + CHUNK - 1).astype(np.int32)


def chunked_rel_attention(q, k, v, rel_bias):
    b, s, _, _ = q.shape
    nc = s // CHUNK
    shp = (b, nc, CHUNK, N_HEADS, HEAD_DIM)
    qc = q.reshape(shp) * (HEAD_DIM ** -0.5)
    pad = ((0, 0), (LEFT_CHUNKS, 0), (0, 0), (0, 0), (0, 0))
    kp = jnp.pad(k.reshape(shp), pad)
    vp = jnp.pad(v.reshape(shp), pad)
    sc = jnp.stack([jnp.einsum('bcqhd,bckhd->bchqk', qc, kp[:, j:j + nc])
                    for j in range(BAND)], axis=4).astype(jnp.float32)
    bias = rel_bias.astype(jnp.float32)[:, rel_index()]
    valid = (jnp.arange(nc)[:, None] + jnp.arange(BAND)[None, :]) >= LEFT_CHUNKS
    sc = jnp.where(valid[None, :, None, None, :, None], sc + bias, NEG_INF)
    p = jax.nn.softmax(sc.reshape(b, nc, N_HEADS, CHUNK, BAND * CHUNK), axis=-1)
    p = p.reshape(b, nc, N_HEADS, CHUNK, BAND, CHUNK).astype(v.dtype)
    out = jnp.einsum('bchqk,bckhd->bcqhd', p[:, :, :, :, 0], vp[:, 0:nc])
    for j in range(1, BAND):
        out = out + jnp.einsum('bchqk,bckhd->bcqhd', p[:, :, :, :, j], vp[:, j:j + nc])
    return out.reshape(b, s, ATT_WIDTH)


def causal_depthwise_conv(x, w, bias):
    s = x.shape[1]
    xp = jnp.pad(x, ((0, 0), (CONV_W - 1, 0), (0, 0)))
    y = bias
    for t in range(CONV_W):
        y = y + xp[:, t:t + s] * w[t]
    return y


def rg_lru(x, w_a, b_a, w_x, b_x, lam):
    b, s, w = x.shape
    xb = x.reshape(b, s, LRU_BLOCKS, LRU_BLOCK_W)
    r = jax.nn.sigmoid(jnp.einsum('bsnc,ncd->bsnd', xb, w_a).reshape(b, s, w) + b_a)
    i = jax.nn.sigmoid(jnp.einsum('bsnc,ncd->bsnd', xb, w_x).reshape(b, s, w) + b_x)
    log_a = -LRU_C * r.astype(jnp.float32) * jax.nn.softplus(-lam.astype(jnp.float32))
    a = jnp.exp(log_a)
    u = jnp.sqrt(-jnp.expm1(2.0 * log_a)) * (i * x).astype(jnp.float32)

    def combine(left, right):
        a1, h1 = left
        a2, h2 = right
        return a1 * a2, a2 * h1 + h2

    _, h = lax.associative_scan(combine, (a, u), axis=1)
    return h.astype(x.dtype)


def swiglu(x, w1, w3, w2):
    return (jax.nn.silu(x @ w1) * (x @ w3)) @ w2


def moe_swiglu(x, w_router, w1, w3, w2):
    b, s, d = x.shape
    xt = x.reshape(-1, d)
    n = xt.shape[0]
    nk = n * TOP_K
    logits = jnp.dot(xt, w_router).astype(jnp.float32)
    top_logit, top_e = lax.top_k(logits, TOP_K)
    gate = jax.nn.softmax(top_logit, axis=-1)
    flat_e = top_e.reshape(-1)
    order = jnp.argsort(flat_e)
    e_sorted = flat_e[order]
    tok_sorted = order // TOP_K
    gate_sorted = gate.reshape(-1)[order]
    counts = jnp.zeros((N_EXPERTS,), jnp.int32).at[flat_e].add(1)
    padded = (counts + EXPERT_BLOCK - 1) // EXPERT_BLOCK * EXPERT_BLOCK
    start = jnp.cumsum(counts) - counts
    pend = jnp.cumsum(padded)
    pstart = pend - padded
    dest = pstart[e_sorted] + (jnp.arange(nk, dtype=jnp.int32) - start[e_sorted])
    n_blocks = -(-nk // EXPERT_BLOCK) + N_EXPERTS
    xs = jnp.zeros((n_blocks * EXPERT_BLOCK, d), x.dtype).at[dest].set(xt[tok_sorted])
    block_e = jnp.minimum(jnp.searchsorted(pend // EXPERT_BLOCK, jnp.arange(n_blocks), side='right'),
                          N_EXPERTS - 1)

    def expert_block(args):
        xb, e = args
        return swiglu(xb, w1[e], w3[e], w2[e])

    ys = lax.map(expert_block, (xs.reshape(n_blocks, EXPERT_BLOCK, d), block_e)).reshape(-1, d)
    y = jnp.zeros_like(xt).at[tok_sorted].add(ys[dest] * gate_sorted[:, None].astype(x.dtype))
    return y.reshape(b, s, d)


def setup_inputs(seed: int = 0) -> dict:
    key = jax.random.key(seed)
    ks = jax.random.split(key, 24)

    def nrm(k, shape, scale):
        return jax.random.normal(k, shape, jnp.float32) * scale

    u = jax.random.uniform(ks[12], (DEPTH, LRU_WIDTH), jnp.float32, minval=0.9, maxval=0.999)
    a0 = u ** (1.0 / LRU_C)
    return {
        "x": nrm(ks[0], (BATCH, SEQ, D_MODEL), 1.0),
        "norm_mix_g": 1.0 + nrm(ks[1], (DEPTH, D_MODEL), 0.01),
        "norm_ffn_g": 1.0 + nrm(ks[2], (DEPTH, D_MODEL), 0.01),
        "norm_final_g": 1.0 + nrm(ks[3], (D_MODEL,), 0.01),
        "w_in": nrm(ks[4], (DEPTH, D_MODEL, IN_WIDTH), D_MODEL ** -0.5),
        "rel_bias": nrm(ks[5], (DEPTH, N_HEADS, N_REL), 0.1),
        "conv_w": nrm(ks[6], (DEPTH, CONV_W, LRU_WIDTH), CONV_W ** -0.5),
        "conv_b": nrm(ks[7], (DEPTH, LRU_WIDTH), 0.01),
        "lru_wa": nrm(ks[8], (DEPTH, LRU_BLOCKS, LRU_BLOCK_W, LRU_BLOCK_W), LRU_BLOCK_W ** -0.5),
        "lru_ba": nrm(ks[9], (DEPTH, LRU_WIDTH), 0.01),
        "lru_wx": nrm(ks[10], (DEPTH, LRU_BLOCKS, LRU_BLOCK_W, LRU_BLOCK_W), LRU_BLOCK_W ** -0.5),
        "lru_bx": nrm(ks[11], (DEPTH, LRU_WIDTH), 0.01),
        "lru_lambda": jnp.log(a0) - jnp.log1p(-a0),
        "w_att_o": nrm(ks[13], (DEPTH, ATT_WIDTH, D_MODEL), ATT_WIDTH ** -0.5),
        "w_rec_o": nrm(ks[14], (DEPTH, LRU_WIDTH, D_MODEL), LRU_WIDTH ** -0.5),
        "w_out": nrm(ks[15], (DEPTH, D_MODEL, D_MODEL), D_MODEL ** -0.5),
        "ffn_w1": nrm(ks[16], (N_DENSE, D_MODEL, D_FF), D_MODEL ** -0.5),
        "ffn_w3": nrm(ks[17], (N_DENSE, D_MODEL, D_FF), D_MODEL ** -0.5),
        "ffn_w2": nrm(ks[18], (N_DENSE, D_FF, D_MODEL), D_FF ** -0.5),
        "router_w": nrm(ks[19], (N_MOE, D_MODEL, N_EXPERTS), D_MODEL ** -0.5),
        "moe_w1": nrm(ks[20], (N_MOE, N_EXPERTS, D_MODEL, D_FF_EXPERT), D_MODEL ** -0.5),
        "moe_w3": nrm(ks[21], (N_MOE, N_EXPERTS, D_MODEL, D_FF_EXPERT), D_MODEL ** -0.5),
        "moe_w2": nrm(ks[22], (N_MOE, N_EXPERTS, D_FF_EXPERT, D_MODEL), D_FF_EXPERT ** -0.5),
    }


def reference(x, norm_mix_g, norm_ffn_g, norm_final_g, w_in, rel_bias, conv_w, conv_b,
              lru_wa, lru_ba, lru_wx, lru_bx, lru_lambda, w_att_o, w_rec_o, w_out,
              ffn_w1, ffn_w3, ffn_w2, router_w, moe_w1, moe_w3, moe_w2):
    b, s, _ = x.shape
    h = x
    for l in range(DEPTH):
        xn = rms_norm(h, norm_mix_g[l])
        cols = xn @ w_in[l]
        q, k, v, rx, ry, ga, gb = jnp.split(cols, SPLIT_POINTS, axis=-1)
        att = chunked_rel_attention(q.reshape(b, s, N_HEADS, HEAD_DIM),
                                    k.reshape(b, s, N_HEADS, HEAD_DIM),
                                    v.reshape(b, s, N_HEADS, HEAD_DIM), rel_bias[l])
        rc = causal_depthwise_conv(rx, conv_w[l], conv_b[l])
        rec = jax.nn.gelu(ry) * rg_lru(rc, lru_wa[l], lru_ba[l], lru_wx[l], lru_bx[l], lru_lambda[l])
        merged = jax.nn.sigmoid(ga) * (att @ w_att_o[l]) + jax.nn.sigmoid(gb) * (rec @ w_rec_o[l])
        h = h + merged @ w_out[l]
        hn = rms_norm(h, norm_ffn_g[l])
        if l % 2 == 0:
            i = l // 2
            h = h + swiglu(hn, ffn_w1[i], ffn_w3[i], ffn_w2[i])
        else:
            i = l // 2
            h = h + moe_swiglu(hn, router_w[i], moe_w1[i], moe_w3[i], moe_w2[i])
    return rms_norm(h, norm_final_g)
```

```python
import functools

import numpy as np
import jax
import jax.numpy as jnp
from jax import lax
from jax.experimental import pallas as pl
from jax.experimental.pallas import tpu as pltpu

CHUNK = 64
LEFT_CHUNKS = 8
N_HEADS = 8
HEAD_DIM = 128
MAX_REL_DIST = 256
LRU_BLOCKS = 8
CONV_W = 4
LRU_C = 8.0
N_EXPERTS = 8
TOP_K = 2
EXPERT_BLOCK = 256
RMS_EPS = 1e-6
NEG_INF = -1e30

LANES = 128
SUBLANES = 8
VMEM_LIMIT_BYTES = 56 * 1024 * 1024

ATT_QBLOCK = 4 * CHUNK
ATT_KBLOCKS = (LEFT_CHUNKS * CHUNK) // ATT_QBLOCK + 1

F32 = jnp.float32
BF16 = jnp.bfloat16


def _cparams(*sem):
    return pltpu.CompilerParams(dimension_semantics=sem, vmem_limit_bytes=VMEM_LIMIT_BYTES)


def _resident(shape, index_map):
    return pl.BlockSpec(shape, index_map, pipeline_mode=pl.Buffered(1))


def _rms(x):
    return x * lax.rsqrt(jnp.mean(x * x, axis=-1, keepdims=True) + RMS_EPS)


def _sigmoid(x):
    return 1.0 / (1.0 + jnp.exp(-x))


def _gelu_tanh(x):
    c = np.float32(np.sqrt(2.0 / np.pi))
    return 0.5 * x * (1.0 + jnp.tanh(c * (x + 0.044715 * (x * x * x))))


def _softplus(x):
    return jnp.maximum(x, 0.0) + jnp.log1p(jnp.exp(-jnp.abs(x)))


def _rmsnorm_kernel(x_ref, g_ref, o_ref):
    o_ref[...] = (_rms(x_ref[...]) * g_ref[...]).astype(o_ref.dtype)


def rmsnorm(x, g, out_dtype, tm=512):
    n, d = x.shape
    return pl.pallas_call(
        _rmsnorm_kernel,
        grid=(n // tm,),
        in_specs=[pl.BlockSpec((tm, d), lambda i: (i, 0)),
                  pl.BlockSpec((1, d), lambda i: (0, 0))],
        out_specs=pl.BlockSpec((tm, d), lambda i: (i, 0)),
        out_shape=jax.ShapeDtypeStruct((n, d), out_dtype),
        compiler_params=_cparams("arbitrary"),
        name="rmsnorm",
    )(x, g.reshape(1, d))


def _inproj_kernel(x_ref, w_ref, o_ref, *, q_scale):
    acc = jnp.dot(x_ref[...], w_ref[...], preferred_element_type=F32)
    s = jnp.where(pl.program_id(1) == 0, np.float32(q_scale), np.float32(1.0))
    o_ref[...] = (acc * s).astype(o_ref.dtype)


def in_projection(xn, w, q_width, q_scale, tm=1024):
    n, d = xn.shape
    width = w.shape[1]
    tn = q_width
    return pl.pallas_call(
        functools.partial(_inproj_kernel, q_scale=q_scale),
        grid=(n // tm, width // tn),
        in_specs=[pl.BlockSpec((tm, d), lambda i, j: (i, 0)),
                  pl.BlockSpec((d, tn), lambda i, j: (0, j))],
        out_specs=pl.BlockSpec((tm, tn), lambda i, j: (i, j)),
        out_shape=jax.ShapeDtypeStruct((n, width), BF16),
        compiler_params=_cparams("arbitrary", "arbitrary"),
        name="in_projection",
    )(xn, w)


def _rel_bias_table(rel_bias):
    qb = ATT_QBLOCK
    kw = ATT_KBLOCKS * ATT_QBLOCK
    qi = np.arange(qb)[:, None]
    ki = np.arange(kw)[None, :]
    dist = qi + LEFT_CHUNKS * CHUNK - ki
    idx = (np.clip(dist, -(CHUNK - 1), MAX_REL_DIST) + CHUNK - 1).astype(np.int32)
    qc = qi // CHUNK
    kc = ki // CHUNK
    valid = (kc >= qc) & (kc <= qc + LEFT_CHUNKS)
    tab = rel_bias.astype(F32)[:, idx]
    return jnp.where(valid[None], tab, NEG_INF)


def _attn_kernel(q_ref, k0_ref, k1_ref, k2_ref, v0_ref, v1_ref, v2_ref, b_ref, o_ref, *, blocks_per_seq):
    i = pl.program_id(0) % blocks_per_seq
    pen0 = jnp.where(i >= 2, np.float32(0.0), np.float32(NEG_INF))
    pen1 = jnp.where(i >= 1, np.float32(0.0), np.float32(NEG_INF))
    qb = ATT_QBLOCK
    nt = (((1,), (1,)), ((), ()))
    for h in range(N_HEADS):
        sl = slice(h * HEAD_DIM, (h + 1) * HEAD_DIM)
        q = q_ref[:, sl]
        s0 = lax.dot_general(q, k0_ref[:, sl], nt, preferred_element_type=F32) + b_ref[h, :, 0:qb] + pen0
        s1 = lax.dot_general(q, k1_ref[:, sl], nt, preferred_element_type=F32) + b_ref[h, :, qb:2 * qb] + pen1
        s2 = lax.dot_general(q, k2_ref[:, sl], nt, preferred_element_type=F32) + b_ref[h, :, 2 * qb:3 * qb]
        m = jnp.max(jnp.maximum(jnp.maximum(s0, s1), s2), axis=-1, keepdims=True)
        p0 = jnp.exp(s0 - m)
        p1 = jnp.exp(s1 - m)
        p2 = jnp.exp(s2 - m)
        l = jnp.sum(p0 + p1 + p2, axis=-1, keepdims=True)
        acc = jnp.dot(p0.astype(BF16), v0_ref[:, sl], preferred_element_type=F32)
        acc = acc + jnp.dot(p1.astype(BF16), v1_ref[:, sl], preferred_element_type=F32)
        acc = acc + jnp.dot(p2.astype(BF16), v2_ref[:, sl], preferred_element_type=F32)
        o_ref[:, sl] = (acc / l).astype(o_ref.dtype)


def chunked_attention(cols, bias_tab, seq):
    n = cols.shape[0]
    qb = ATT_QBLOCK
    aw = N_HEADS * HEAD_DIM
    bps = seq // qb
    assert ATT_KBLOCKS == 3 and seq % qb == 0

    def kv_spec(col, back):
        return pl.BlockSpec((qb, aw), lambda g: (g - jnp.minimum(g % bps, back), col))

    return pl.pallas_call(
        functools.partial(_attn_kernel, blocks_per_seq=bps),
        grid=(n // qb,),
        in_specs=[pl.BlockSpec((qb, aw), lambda g: (g, 0)),
                  kv_spec(1, 2), kv_spec(1, 1), kv_spec(1, 0),
                  kv_spec(2, 2), kv_spec(2, 1), kv_spec(2, 0),
                  _resident(bias_tab.shape, lambda g: (0, 0, 0))],
        out_specs=pl.BlockSpec((qb, aw), lambda g: (g, 0)),
        out_shape=jax.ShapeDtypeStruct((n, aw), BF16),
        compiler_params=_cparams("arbitrary"),
        name="chunked_attention",
    )(cols, cols, cols, cols, cols, cols, cols, bias_tab)


def _rec_kernel(rx_ref, ry_ref, cw_ref, cb_ref, wg_ref, ba_ref, bx_ref, lam_ref, o_ref,
                xbuf, a_s, u_s, carry, *, rows, width, bw):
    t = pl.program_id(1)
    pad = SUBLANES

    @pl.when(t == 0)
    def _():
        xbuf[0:pad, :] = jnp.zeros((pad, width), F32)
        carry[...] = jnp.zeros((pad, width), F32)

    x = rx_ref[...].astype(F32)
    xbuf[pad:pad + rows, :] = x
    rc = cb_ref[...] + cw_ref[0:1, :] * xbuf[pad - 3:pad - 3 + rows, :]
    rc = rc + cw_ref[1:2, :] * xbuf[pad - 2:pad - 2 + rows, :]
    rc = rc + cw_ref[2:3, :] * xbuf[pad - 1:pad - 1 + rows, :]
    rc = rc + cw_ref[3:4, :] * x
    xbuf[0:pad, :] = xbuf[rows:rows + pad, :]
    u_s[...] = rc

    neg_c_sp = -LRU_C * _softplus(-lam_ref[...])
    for nb in range(width // bw):
        sl = slice(nb * bw, (nb + 1) * bw)
        rcn = u_s[:, sl]
        g = jnp.dot(rcn.astype(BF16), wg_ref[nb], preferred_element_type=F32)
        r = _sigmoid(g[:, :bw] + ba_ref[:, sl])
        ig = _sigmoid(g[:, bw:] + bx_ref[:, sl])
        log_a = neg_c_sp[:, sl] * r
        a_s[:, sl] = jnp.exp(log_a)
        th = jnp.tanh(log_a)
        u_s[:, sl] = jnp.sqrt(-2.0 * th / (1.0 - th)) * (ig * rcn)

    row = lax.broadcasted_iota(jnp.int32, (pad, width), 0)

    def group(gi, c):
        r0 = pl.multiple_of(gi * pad, pad)
        a = a_s[pl.ds(r0, pad), :]
        h = u_s[pl.ds(r0, pad), :]
        for s in (1, 2, 4):
            keep = row >= s
            a_prev = jnp.where(keep, pltpu.roll(a, s, 0), 1.0)
            h_prev = jnp.where(keep, pltpu.roll(h, s, 0), 0.0)
            h = a * h_prev + h
            a = a * a_prev
        h = h + a * c
        u_s[pl.ds(r0, pad), :] = h
        return jnp.broadcast_to(h[pad - 1:pad, :], (pad, width))

    carry[...] = lax.fori_loop(0, rows // pad, group, carry[...])
    o_ref[...] = (_gelu_tanh(ry_ref[...].astype(F32)) * u_s[...]).astype(o_ref.dtype)


def recurrent_branch(cols, conv_w, conv_b, w_gates, b_a, b_x, lam, batch, seq, rx_col, rows=256):
    n = cols.shape[0]
    width = conv_w.shape[1]
    bw = width // LRU_BLOCKS
    spb = seq // rows
    vec = lambda: pl.BlockSpec((1, width), lambda b, t: (0, 0))
    return pl.pallas_call(
        functools.partial(_rec_kernel, rows=rows, width=width, bw=bw),
        grid=(batch, spb),
        in_specs=[pl.BlockSpec((rows, width), lambda b, t: (b * spb + t, rx_col)),
                  pl.BlockSpec((rows, width), lambda b, t: (b * spb + t, rx_col + 1)),
                  pl.BlockSpec((CONV_W, width), lambda b, t: (0, 0)),
                  vec(),
                  pl.BlockSpec((LRU_BLOCKS, bw, 2 * bw), lambda b, t: (0, 0, 0)),
                  vec(), vec(), vec()],
        out_specs=pl.BlockSpec((rows, width), lambda b, t: (b * spb + t, 0)),
        out_shape=jax.ShapeDtypeStruct((n, width), BF16),
        scratch_shapes=[pltpu.VMEM((rows + SUBLANES, width), F32),
                        pltpu.VMEM((rows, width), F32),
                        pltpu.VMEM((rows, width), F32),
                        pltpu.VMEM((SUBLANES, width), F32)],
        compiler_params=_cparams("arbitrary", "arbitrary"),
        name="recurrent_branch",
    )(cols, cols, conv_w, conv_b.reshape(1, width), w_gates,
      b_a.reshape(1, width), b_x.reshape(1, width), lam.reshape(1, width))


def _merge_kernel(att_ref, rec_ref, wa_ref, wr_ref, ga_ref, gb_ref, o_ref):
    ya = jnp.dot(att_ref[...], wa_ref[...], preferred_element_type=F32)
    yr = jnp.dot(rec_ref[...], wr_ref[...], preferred_element_type=F32)
    o = _sigmoid(ga_ref[...].astype(F32)) * ya + _sigmoid(gb_ref[...].astype(F32)) * yr
    o_ref[...] = o.astype(o_ref.dtype)


def gated_merge(att, rec, w_att_o, w_rec_o, cols, ga_col, tm=1024, tn=1024):
    n, aw = att.shape
    rw = rec.shape[1]
    d = w_att_o.shape[1]
    nj = d // tn
    return pl.pallas_call(
        _merge_kernel,
        grid=(n // tm, nj),
        in_specs=[pl.BlockSpec((tm, aw), lambda i, j: (i, 0)),
                  pl.BlockSpec((tm, rw), lambda i, j: (i, 0)),
                  pl.BlockSpec((aw, tn), lambda i, j: (0, j)),
                  pl.BlockSpec((rw, tn), lambda i, j: (0, j)),
                  pl.BlockSpec((tm, tn), lambda i, j: (i, ga_col + j)),
                  pl.BlockSpec((tm, tn), lambda i, j: (i, ga_col + nj + j))],
        out_specs=pl.BlockSpec((tm, tn), lambda i, j: (i, j)),
        out_shape=jax.ShapeDtypeStruct((n, d), BF16),
        compiler_params=_cparams("arbitrary", "arbitrary"),
        name="gated_merge",
    )(att, rec, w_att_o, w_rec_o, cols, cols)


def _proj_res_norm_kernel(x_ref, w_ref, h_ref, g_ref, hout_ref, n_ref, *extra):
    hn = h_ref[...] + jnp.dot(x_ref[...], w_ref[...], preferred_element_type=F32)
    hout_ref[...] = hn
    nrm = _rms(hn) * g_ref[...]
    n_ref[...] = nrm.astype(n_ref.dtype)
    for r in extra:
        r[...] = nrm.astype(r.dtype)


def proj_res_norm(x, w, h, g, norm_dtypes, tm=256):
    n, k = x.shape
    d = w.shape[1]
    row = lambda: pl.BlockSpec((tm, d), lambda i: (i, 0))
    return pl.pallas_call(
        _proj_res_norm_kernel,
        grid=(n // tm,),
        in_specs=[pl.BlockSpec((tm, k), lambda i: (i, 0)),
                  _resident((k, d), lambda i: (0, 0)),
                  row(),
                  pl.BlockSpec((1, d), lambda i: (0, 0))],
        out_specs=[row()] + [row() for _ in norm_dtypes],
        out_shape=[jax.ShapeDtypeStruct((n, d), F32)] + [jax.ShapeDtypeStruct((n, d), dt) for dt in norm_dtypes],
        compiler_params=_cparams("arbitrary"),
        name="proj_res_norm",
    )(x, w, h, g.reshape(1, d))


def _ffn_up_kernel(x_ref, w1_ref, w3_ref, o_ref):
    x = x_ref[...]
    a = jnp.dot(x, w1_ref[...], preferred_element_type=F32)
    b = jnp.dot(x, w3_ref[...], preferred_element_type=F32)
    o_ref[...] = ((a * _sigmoid(a)) * b).astype(o_ref.dtype)


def ffn_up(xn, w1, w3, tm=1024, tn=512):
    n, d = xn.shape
    f = w1.shape[1]
    return pl.pallas_call(
        _ffn_up_kernel,
        grid=(n // tm, f // tn),
        in_specs=[pl.BlockSpec((tm, d), lambda i, j: (i, 0)),
                  pl.BlockSpec((d, tn), lambda i, j: (0, j)),
                  pl.BlockSpec((d, tn), lambda i, j: (0, j))],
        out_specs=pl.BlockSpec((tm, tn), lambda i, j: (i, j)),
        out_shape=jax.ShapeDtypeStruct((n, f), BF16),
        compiler_params=_cparams("arbitrary", "arbitrary"),
        name="ffn_up",
    )(xn, w1, w3)


ROUTE_E0, ROUTE_E1, ROUTE_G0, ROUTE_G1, ROUTE_R0, ROUTE_R1 = range(6)


def _router_kernel(x_ref, w_ref, tri_ref, route_ref, cnt_ref, carry):
    @pl.when(pl.program_id(0) == 0)
    def _():
        carry[...] = jnp.zeros_like(carry)

    tm = x_ref.shape[0]
    logits = jnp.dot(x_ref[...], w_ref[...], preferred_element_type=F32)
    lane = lax.broadcasted_iota(jnp.int32, (tm, LANES), 1).astype(F32)
    neg = np.float32(-np.inf)
    lg = jnp.where(lane < N_EXPERTS, logits, neg)
    m0 = jnp.max(lg, axis=-1, keepdims=True)
    e0 = jnp.min(jnp.where(lg == m0, lane, np.float32(LANES)), axis=-1, keepdims=True)
    lg1 = jnp.where(lane == e0, neg, lg)
    m1 = jnp.max(lg1, axis=-1, keepdims=True)
    e1 = jnp.min(jnp.where(lg1 == m1, lane, np.float32(LANES)), axis=-1, keepdims=True)
    ex = jnp.exp(m1 - m0)
    g0 = 1.0 / (1.0 + ex)
    g1 = ex / (1.0 + ex)
    sel0 = lane == e0
    sel1 = lane == e1
    member = jnp.where(sel0 | sel1, np.float32(1.0), np.float32(0.0))
    incl = jnp.dot(tri_ref[...], member.astype(BF16), preferred_element_type=F32)
    before = carry[...] + incl - member
    r0 = jnp.sum(jnp.where(sel0, before, 0.0), axis=-1, keepdims=True)
    r1 = jnp.sum(jnp.where(sel1, before, 0.0), axis=-1, keepdims=True)
    total = carry[...] + incl[tm - 1:tm, :]
    carry[...] = total
    cnt_ref[...] = jnp.broadcast_to(total, cnt_ref.shape)
    out = jnp.zeros((tm, LANES), F32)
    for k, v in ((ROUTE_E0, e0), (ROUTE_E1, e1), (ROUTE_G0, g0), (ROUTE_G1, g1), (ROUTE_R0, r0), (ROUTE_R1, r1)):
        out = jnp.where(lane == k, v, out)
    route_ref[...] = out


def router(hn, w_router, tm=512):
    n, d = hn.shape
    wr = jnp.zeros((d, LANES), BF16).at[:, :N_EXPERTS].set(w_router.astype(BF16))
    tri = jnp.asarray(np.tril(np.ones((tm, tm), np.float32)), BF16)
    return pl.pallas_call(
        _router_kernel,
        grid=(n // tm,),
        in_specs=[pl.BlockSpec((tm, d), lambda i: (i, 0)),
                  pl.BlockSpec((d, LANES), lambda i: (0, 0)),
                  pl.BlockSpec((tm, tm), lambda i: (0, 0))],
        out_specs=[pl.BlockSpec((tm, LANES), lambda i: (i, 0)),
                   pl.BlockSpec((SUBLANES, LANES), lambda i: (0, 0))],
        out_shape=[jax.ShapeDtypeStruct((n, LANES), F32),
                   jax.ShapeDtypeStruct((SUBLANES, LANES), F32)],
        scratch_shapes=[pltpu.VMEM((1, LANES), F32)],
        compiler_params=_cparams("arbitrary"),
        name="router",
    )(hn, wr, tri)


def _dispatch_kernel(dest_ref, zero_ref, x_ref, xs_ref, zbuf, sem, zsem, *, tm):
    step = pl.program_id(0)
    eb = EXPERT_BLOCK

    @pl.when(step == 0)
    def _():
        zbuf[...] = jnp.zeros_like(zbuf)
        for z in range(zero_ref.shape[0]):
            @pl.when(zero_ref[z] >= 0)
            def _():
                cp = pltpu.make_async_copy(zbuf, xs_ref.at[pl.ds(pl.multiple_of(zero_ref[z], eb), eb)], zsem)
                cp.start()
                cp.wait()

    base = step * (tm * TOP_K)

    def row_copy(r, d):
        return pltpu.make_async_copy(x_ref.at[pl.ds(r, 1)], xs_ref.at[pl.ds(d, 1)], sem)

    def issue(r, carry):
        row_copy(r, dest_ref[base + TOP_K * r]).start()
        row_copy(r, dest_ref[base + TOP_K * r + 1]).start()
        return carry

    lax.fori_loop(0, tm, issue, 0)
    for _ in range(TOP_K * tm):
        row_copy(0, 0).wait()


def dispatch(hn32, dest, zero_start, n_rows, tm=256):
    n, d = hn32.shape
    return pl.pallas_call(
        functools.partial(_dispatch_kernel, tm=tm),
        grid_spec=pltpu.PrefetchScalarGridSpec(
            num_scalar_prefetch=2,
            grid=(n // tm,),
            in_specs=[pl.BlockSpec((tm, d), lambda i, dest, tail: (i, 0))],
            out_specs=pl.BlockSpec(memory_space=pl.ANY),
            scratch_shapes=[pltpu.VMEM((EXPERT_BLOCK, d), F32),
                            pltpu.SemaphoreType.DMA,
                            pltpu.SemaphoreType.DMA]),
        out_shape=jax.ShapeDtypeStruct((n_rows, d), F32),
        compiler_params=_cparams("arbitrary"),
        name="moe_dispatch",
    )(dest, zero_start, hn32)


def _used_block(i, nu):
    return jnp.minimum(i, nu[0] - 1)


def _moe_up_kernel(be_ref, nu_ref, x_ref, w1_ref, w3_ref, o_ref):
    used = pl.program_id(1) < nu_ref[0]

    @pl.when(used)
    def _():
        x = x_ref[...].astype(BF16)
        a = jnp.dot(x, w1_ref[0], preferred_element_type=F32)
        b = jnp.dot(x, w3_ref[0], preferred_element_type=F32)
        o_ref[...] = ((a * _sigmoid(a)) * b).astype(o_ref.dtype)

    @pl.when(jnp.logical_not(used))
    def _():
        o_ref[...] = jnp.zeros_like(o_ref)


def moe_up(xs, w1, w3, block_e, n_used, tn=1024):
    rows, d = xs.shape
    f = w1.shape[2]
    tm = EXPERT_BLOCK
    return pl.pallas_call(
        _moe_up_kernel,
        grid_spec=pltpu.PrefetchScalarGridSpec(
            num_scalar_prefetch=2,
            grid=(f // tn, rows // tm),
            in_specs=[pl.BlockSpec((tm, d), lambda j, i, be, nu: (_used_block(i, nu), 0)),
                      pl.BlockSpec((1, d, tn), lambda j, i, be, nu: (be[_used_block(i, nu)], 0, j)),
                      pl.BlockSpec((1, d, tn), lambda j, i, be, nu: (be[_used_block(i, nu)], 0, j))],
            out_specs=pl.BlockSpec((tm, tn), lambda j, i, be, nu: (i, j))),
        out_shape=jax.ShapeDtypeStruct((rows, f), BF16),
        compiler_params=_cparams("arbitrary", "arbitrary"),
        name="moe_up",
    )(block_e, n_used, xs, w1, w3)


def _moe_down_kernel(be_ref, nu_ref, a_ref, w2_ref, o_ref):
    used = pl.program_id(1) < nu_ref[0]

    @pl.when(used)
    def _():
        o_ref[...] = jnp.dot(a_ref[...], w2_ref[0], preferred_element_type=F32)

    @pl.when(jnp.logical_not(used))
    def _():
        o_ref[...] = jnp.zeros_like(o_ref)


def moe_down(act, w2, block_e, n_used, tn=512):
    rows, f = act.shape
    d = w2.shape[2]
    tm = EXPERT_BLOCK
    return pl.pallas_call(
        _moe_down_kernel,
        grid_spec=pltpu.PrefetchScalarGridSpec(
            num_scalar_prefetch=2,
            grid=(d // tn, rows // tm),
            in_specs=[pl.BlockSpec((tm, f), lambda j, i, be, nu: (_used_block(i, nu), 0)),
                      pl.BlockSpec((1, f, tn), lambda j, i, be, nu: (be[_used_block(i, nu)], 0, j))],
            out_specs=pl.BlockSpec((tm, tn), lambda j, i, be, nu: (i, j))),
        out_shape=jax.ShapeDtypeStruct((rows, d), F32),
        compiler_params=_cparams("arbitrary", "arbitrary"),
        name="moe_down",
    )(block_e, n_used, act, w2)


def _combine_kernel(dest_ref, ys_ref, route_ref, h_ref, g_ref, hout_ref, n_ref, buf0, buf1, sems, *, tm):
    base = pl.program_id(0) * (tm * TOP_K)

    def row_copy(buf, slot, r, d):
        return pltpu.make_async_copy(ys_ref.at[pl.ds(d, 1)], buf.at[pl.ds(r, 1)], sems.at[slot])

    def issue(r, carry):
        row_copy(buf0, 0, r, dest_ref[base + TOP_K * r]).start()
        row_copy(buf1, 1, r, dest_ref[base + TOP_K * r + 1]).start()
        return carry

    lax.fori_loop(0, tm, issue, 0)
    for _ in range(tm):
        row_copy(buf0, 0, 0, 0).wait()
    for _ in range(tm):
        row_copy(buf1, 1, 0, 0).wait()

    route = route_ref[...]
    g0 = route[:, ROUTE_G0:ROUTE_G0 + 1]
    g1 = route[:, ROUTE_G1:ROUTE_G1 + 1]
    hn = h_ref[...] + (g0 * buf0[...] + g1 * buf1[...])
    hout_ref[...] = hn
    n_ref[...] = (_rms(hn) * g_ref[...]).astype(n_ref.dtype)


def combine(ys, dest, route, h, g, norm_dtype, tm=256):
    n, d = h.shape
    row = lambda: pl.BlockSpec((tm, d), lambda i, dest: (i, 0))
    return pl.pallas_call(
        functools.partial(_combine_kernel, tm=tm),
        grid_spec=pltpu.PrefetchScalarGridSpec(
            num_scalar_prefetch=1,
            grid=(n // tm,),
            in_specs=[pl.BlockSpec(memory_space=pl.ANY),
                      pl.BlockSpec((tm, LANES), lambda i, dest: (i, 0)),
                      row(),
                      pl.BlockSpec((1, d), lambda i, dest: (0, 0))],
            out_specs=[row(), row()],
            scratch_shapes=[pltpu.VMEM((tm, d), F32),
                            pltpu.VMEM((tm, d), F32),
                            pltpu.SemaphoreType.DMA((TOP_K,))]),
        out_shape=[jax.ShapeDtypeStruct((n, d), F32), jax.ShapeDtypeStruct((n, d), norm_dtype)],
        compiler_params=_cparams("arbitrary"),
        name="moe_combine",
    )(dest, ys, route, h, g.reshape(1, d))


def moe_layer(hn_bf, hn32, h, w_router, w1, w3, w2, g_next, norm_dtype):
    n = h.shape[0]
    eb = EXPERT_BLOCK
    route, cnt = router(hn_bf, w_router)
    counts = cnt[0, :N_EXPERTS].astype(jnp.int32)
    padded = (counts + eb - 1) // eb * eb
    pend = jnp.cumsum(padded)
    pstart = pend - padded
    experts = route[:, ROUTE_E0:ROUTE_E1 + 1].astype(jnp.int32)
    ranks = route[:, ROUTE_R0:ROUTE_R1 + 1].astype(jnp.int32)
    dest = (pstart[experts] + ranks).reshape(-1)
    n_blocks = -(-(n * TOP_K) // eb) + N_EXPERTS
    block_e = jnp.minimum(jnp.searchsorted(pend // eb, jnp.arange(n_blocks), side="right"),
                          N_EXPERTS - 1).astype(jnp.int32)
    n_used = (pend[-1:] // eb).astype(jnp.int32)
    tail_start = jnp.where(padded > 0, pend - eb, -1)
    spare = (n_used[0] + jnp.arange(N_EXPERTS)) * eb
    spare_start = jnp.where(spare < n_blocks * eb, spare, -1)
    zero_start = jnp.concatenate([tail_start, spare_start]).astype(jnp.int32)
    xs = dispatch(hn32, dest, zero_start, n_blocks * eb)
    act = moe_up(xs, w1, w3, block_e, n_used)
    ys = moe_down(act, w2, block_e, n_used)
    return combine(ys, dest, route, h, g_next, norm_dtype)


def kernel(x, norm_mix_g, norm_ffn_g, norm_final_g, w_in, rel_bias, conv_w, conv_b, lru_wa, lru_ba, lru_wx, lru_bx, lru_lambda, w_att_o, w_rec_o, w_out, ffn_w1, ffn_w3, ffn_w2, router_w, moe_w1, moe_w3, moe_w2):
    b, s, d = x.shape
    n = b * s
    depth = w_in.shape[0]
    aw = N_HEADS * HEAD_DIM
    lw = conv_w.shape[2]
    assert lw == aw and d % aw == 0, "column-block indexing assumes equal branch widths"
    h = x.reshape(n, d)
    xn = rmsnorm(h, norm_mix_g[0], BF16)
    out = None
    for l in range(depth):
        cols = in_projection(xn, w_in[l].astype(BF16), aw, HEAD_DIM ** -0.5)
        att = chunked_attention(cols, _rel_bias_table(rel_bias[l]), s)
        w_gates = jnp.concatenate([lru_wa[l], lru_wx[l]], axis=-1).astype(BF16)
        rec = recurrent_branch(cols, conv_w[l], conv_b[l], w_gates, lru_ba[l], lru_bx[l], lru_lambda[l],
                               b, s, rx_col=3)
        merged = gated_merge(att, rec, w_att_o[l].astype(BF16), w_rec_o[l].astype(BF16), cols, ga_col=5)
        last = l == depth - 1
        g_next = norm_final_g if last else norm_mix_g[l + 1]
        next_dtype = F32 if last else BF16
        i = l // 2
        if l % 2 == 0:
            h, hn = proj_res_norm(merged, w_out[l].astype(BF16), h, norm_ffn_g[l], [BF16])
            act = ffn_up(hn, ffn_w1[i].astype(BF16), ffn_w3[i].astype(BF16))
            h, xn = proj_res_norm(act, ffn_w2[i].astype(BF16), h, g_next, [next_dtype])
        else:
            h, hn, hn32 = proj_res_norm(merged, w_out[l].astype(BF16), h, norm_ffn_g[l], [BF16, F32])
            h, xn = moe_layer(hn, hn32, h, router_w[i], moe_w1[i].astype(BF16), moe_w3[i].astype(BF16),
                              moe_w2[i].astype(BF16), g_next, next_dtype)
        out = xn
    return out.reshape(b, s, d)
```

```python
import functools

import numpy as np
import jax
import jax.numpy as jnp
from jax import lax
from jax.experimental import pallas as pl
from jax.experimental.pallas import tpu as pltpu

CHUNK = 64
LEFT_CHUNKS = 8
N_HEADS = 8
HEAD_DIM = 128
MAX_REL_DIST = 256
LRU_BLOCKS = 8
CONV_W = 4
LRU_C = 8.0
N_EXPERTS = 8
TOP_K = 2
EXPERT_BLOCK = 256
RMS_EPS = 1e-6
NEG_INF = -1e30

LANES = 128
SUBLANES = 8
VMEM_LIMIT_BYTES = 56 * 1024 * 1024

ATT_QBLOCK = 4 * CHUNK
ATT_KBLOCKS = (LEFT_CHUNKS * CHUNK) // ATT_QBLOCK + 1

F32 = jnp.float32
BF16 = jnp.bfloat16


def _cparams(*sem):
    return pltpu.CompilerParams(dimension_semantics=sem, vmem_limit_bytes=VMEM_LIMIT_BYTES)


def _resident(shape, index_map):
    return pl.BlockSpec(shape, index_map, pipeline_mode=pl.Buffered(1))


def _rms(x):
    return x * lax.rsqrt(jnp.mean(x * x, axis=-1, keepdims=True) + RMS_EPS)


def _sigmoid(x):
    return 1.0 / (1.0 + jnp.exp(-x))


def _gelu_tanh(x):
    c = np.float32(np.sqrt(2.0 / np.pi))
    return 0.5 * x * (1.0 + jnp.tanh(c * (x + 0.044715 * (x * x * x))))


def _softplus(x):
    return jnp.maximum(x, 0.0) + jnp.log1p(jnp.exp(-jnp.abs(x)))


def _rmsnorm_kernel(x_ref, g_ref, o_ref):
    o_ref[...] = (_rms(x_ref[...]) * g_ref[...]).astype(o_ref.dtype)


def rmsnorm(x, g, out_dtype, tm=512):
    n, d = x.shape
    return pl.pallas_call(
        _rmsnorm_kernel,
        grid=(n // tm,),
        in_specs=[pl.BlockSpec((tm, d), lambda i: (i, 0)),
                  pl.BlockSpec((1, d), lambda i: (0, 0))],
        out_specs=pl.BlockSpec((tm, d), lambda i: (i, 0)),
        out_shape=jax.ShapeDtypeStruct((n, d), out_dtype),
        compiler_params=_cparams("arbitrary"),
        name="rmsnorm",
    )(x, g.reshape(1, d))


def _inproj_kernel(x_ref, w_ref, o_ref, *, q_scale):
    acc = jnp.dot(x_ref[...], w_ref[...], preferred_element_type=F32)
    s = jnp.where(pl.program_id(1) == 0, np.float32(q_scale), np.float32(1.0))
    o_ref[...] = (acc * s).astype(o_ref.dtype)


def in_projection(xn, w, q_width, q_scale, tm=1024):
    n, d = xn.shape
    width = w.shape[1]
    tn = q_width
    return pl.pallas_call(
        functools.partial(_inproj_kernel, q_scale=q_scale),
        grid=(n // tm, width // tn),
        in_specs=[pl.BlockSpec((tm, d), lambda i, j: (i, 0)),
                  pl.BlockSpec((d, tn), lambda i, j: (0, j))],
        out_specs=pl.BlockSpec((tm, tn), lambda i, j: (i, j)),
        out_shape=jax.ShapeDtypeStruct((n, width), BF16),
        compiler_params=_cparams("arbitrary", "arbitrary"),
        name="in_projection",
    )(xn, w)


def _rel_bias_table(rel_bias):
    qb = ATT_QBLOCK
    kw = ATT_KBLOCKS * ATT_QBLOCK
    n_heads, n_rel = rel_bias.shape
    period = qb + kw
    dist = LEFT_CHUNKS * CHUNK + (qb - 1) - np.arange(period)
    idx = np.clip(dist, -(CHUNK - 1), MAX_REL_DIST) + CHUNK - 1
    n_far = int(np.sum(idx == n_rel - 1)) - 1
    n_near = int(np.sum(idx == 0)) - 1
    assert np.array_equal(idx, np.concatenate([np.full(n_far, n_rel - 1), np.arange(n_rel)[::-1], np.full(n_near, 0)]))
    rb = rel_bias.astype(F32)
    diag = jnp.concatenate([jnp.broadcast_to(rb[:, -1:], (n_heads, n_far)), rb[:, ::-1],
                            jnp.broadcast_to(rb[:, :1], (n_heads, n_near))], axis=1)
    rep = jnp.broadcast_to(diag[:, None, :], (n_heads, qb, period)).reshape(n_heads, qb * period)
    skew = rep[:, :qb * (period - 1)].reshape(n_heads, qb, period - 1)
    tab = skew[:, :, qb - 1:qb - 1 + kw]
    qc = np.arange(qb)[:, None] // CHUNK
    kc = np.arange(kw)[None, :] // CHUNK
    valid = (kc >= qc) & (kc <= qc + LEFT_CHUNKS)
    return jnp.where(valid[None], tab, NEG_INF)


def _attn_kernel(q_ref, k0_ref, k1_ref, k2_ref, v0_ref, v1_ref, v2_ref, b_ref, o_ref, *, blocks_per_seq):
    i = pl.program_id(0) % blocks_per_seq
    pen0 = jnp.where(i >= 2, np.float32(0.0), np.float32(NEG_INF))
    pen1 = jnp.where(i >= 1, np.float32(0.0), np.float32(NEG_INF))
    qb = ATT_QBLOCK
    nt = (((1,), (1,)), ((), ()))
    for h in range(N_HEADS):
        sl = slice(h * HEAD_DIM, (h + 1) * HEAD_DIM)
        q = q_ref[:, sl]
        s0 = lax.dot_general(q, k0_ref[:, sl], nt, preferred_element_type=F32) + b_ref[h, :, 0:qb] + pen0
        s1 = lax.dot_general(q, k1_ref[:, sl], nt, preferred_element_type=F32) + b_ref[h, :, qb:2 * qb] + pen1
        s2 = lax.dot_general(q, k2_ref[:, sl], nt, preferred_element_type=F32) + b_ref[h, :, 2 * qb:3 * qb]
        m = jnp.max(jnp.maximum(jnp.maximum(s0, s1), s2), axis=-1, keepdims=True)
        p0 = jnp.exp(s0 - m)
        p1 = jnp.exp(s1 - m)
        p2 = jnp.exp(s2 - m)
        l = jnp.sum(p0 + p1 + p2, axis=-1, keepdims=True)
        acc = jnp.dot(p0.astype(BF16), v0_ref[:, sl], preferred_element_type=F32)
        acc = acc + jnp.dot(p1.astype(BF16), v1_ref[:, sl], preferred_element_type=F32)
        acc = acc + jnp.dot(p2.astype(BF16), v2_ref[:, sl], preferred_element_type=F32)
        o_ref[:, sl] = (acc / l).astype(o_ref.dtype)


def chunked_attention(cols, bias_tab, seq):
    n = cols.shape[0]
    qb = ATT_QBLOCK
    aw = N_HEADS * HEAD_DIM
    bps = seq // qb
    assert ATT_KBLOCKS == 3 and seq % qb == 0

    def kv_spec(col, back):
        return pl.BlockSpec((qb, aw), lambda g: (g - jnp.minimum(g % bps, back), col))

    return pl.pallas_call(
        functools.partial(_attn_kernel, blocks_per_seq=bps),
        grid=(n // qb,),
        in_specs=[pl.BlockSpec((qb, aw), lambda g: (g, 0)),
                  kv_spec(1, 2), kv_spec(1, 1), kv_spec(1, 0),
                  kv_spec(2, 2), kv_spec(2, 1), kv_spec(2, 0),
                  _resident(bias_tab.shape, lambda g: (0, 0, 0))],
        out_specs=pl.BlockSpec((qb, aw), lambda g: (g, 0)),
        out_shape=jax.ShapeDtypeStruct((n, aw), BF16),
        compiler_params=_cparams("arbitrary"),
        name="chunked_attention",
    )(cols, cols, cols, cols, cols, cols, cols, bias_tab)


def _rec_kernel(rx_ref, ry_ref, cw_ref, cb_ref, wg_ref, ba_ref, bx_ref, lam_ref, o_ref,
                xbuf, a_s, u_s, carry, *, rows, width, bw):
    t = pl.program_id(1)
    pad = SUBLANES

    @pl.when(t == 0)
    def _():
        xbuf[0:pad, :] = jnp.zeros((pad, width), F32)
        carry[...] = jnp.zeros((pad, width), F32)

    x = rx_ref[...].astype(F32)
    xbuf[pad:pad + rows, :] = x
    rc = cb_ref[...] + cw_ref[0:1, :] * xbuf[pad - 3:pad - 3 + rows, :]
    rc = rc + cw_ref[1:2, :] * xbuf[pad - 2:pad - 2 + rows, :]
    rc = rc + cw_ref[2:3, :] * xbuf[pad - 1:pad - 1 + rows, :]
    rc = rc + cw_ref[3:4, :] * x
    xbuf[0:pad, :] = xbuf[rows:rows + pad, :]
    u_s[...] = rc

    neg_c_sp = -LRU_C * _softplus(-lam_ref[...])
    for nb in range(width // bw):
        sl = slice(nb * bw, (nb + 1) * bw)
        rcn = u_s[:, sl]
        g = jnp.dot(rcn.astype(BF16), wg_ref[nb], preferred_element_type=F32)
        r = _sigmoid(g[:, :bw] + ba_ref[:, sl])
        ig = _sigmoid(g[:, bw:] + bx_ref[:, sl])
        log_a = neg_c_sp[:, sl] * r
        a_s[:, sl] = jnp.exp(log_a)
        th = jnp.tanh(log_a)
        u_s[:, sl] = jnp.sqrt(-2.0 * th / (1.0 - th)) * (ig * rcn)

    row = lax.broadcasted_iota(jnp.int32, (pad, width), 0)

    def group(gi, c):
        r0 = pl.multiple_of(gi * pad, pad)
        a = a_s[pl.ds(r0, pad), :]
        h = u_s[pl.ds(r0, pad), :]
        for s in (1, 2, 4):
            keep = row >= s
            a_prev = jnp.where(keep, pltpu.roll(a, s, 0), 1.0)
            h_prev = jnp.where(keep, pltpu.roll(h, s, 0), 0.0)
            h = a * h_prev + h
            a = a * a_prev
        h = h + a * c
        u_s[pl.ds(r0, pad), :] = h
        return jnp.broadcast_to(h[pad - 1:pad, :], (pad, width))

    carry[...] = lax.fori_loop(0, rows // pad, group, carry[...])
    o_ref[...] = (_gelu_tanh(ry_ref[...].astype(F32)) * u_s[...]).astype(o_ref.dtype)


def recurrent_branch(cols, conv_w, conv_b, w_gates, b_a, b_x, lam, batch, seq, rx_col, rows=256):
    n = cols.shape[0]
    width = conv_w.shape[1]
    bw = width // LRU_BLOCKS
    spb = seq // rows
    vec = lambda: pl.BlockSpec((1, width), lambda b, t: (0, 0))
    return pl.pallas_call(
        functools.partial(_rec_kernel, rows=rows, width=width, bw=bw),
        grid=(batch, spb),
        in_specs=[pl.BlockSpec((rows, width), lambda b, t: (b * spb + t, rx_col)),
                  pl.BlockSpec((rows, width), lambda b, t: (b * spb + t, rx_col + 1)),
                  pl.BlockSpec((CONV_W, width), lambda b, t: (0, 0)),
                  vec(),
                  pl.BlockSpec((LRU_BLOCKS, bw, 2 * bw), lambda b, t: (0, 0, 0)),
                  vec(), vec(), vec()],
        out_specs=pl.BlockSpec((rows, width), lambda b, t: (b * spb + t, 0)),
        out_shape=jax.ShapeDtypeStruct((n, width), BF16),
        scratch_shapes=[pltpu.VMEM((rows + SUBLANES, width), F32),
                        pltpu.VMEM((rows, width), F32),
                        pltpu.VMEM((rows, width), F32),
                        pltpu.VMEM((SUBLANES, width), F32)],
        compiler_params=_cparams("arbitrary", "arbitrary"),
        name="recurrent_branch",
    )(cols, cols, conv_w, conv_b.reshape(1, width), w_gates,
      b_a.reshape(1, width), b_x.reshape(1, width), lam.reshape(1, width))


def _merge_kernel(att_ref, rec_ref, wa_ref, wr_ref, ga_ref, gb_ref, o_ref):
    ya = jnp.dot(att_ref[...], wa_ref[...], preferred_element_type=F32)
    yr = jnp.dot(rec_ref[...], wr_ref[...], preferred_element_type=F32)
    o = _sigmoid(ga_ref[...].astype(F32)) * ya + _sigmoid(gb_ref[...].astype(F32)) * yr
    o_ref[...] = o.astype(o_ref.dtype)


def gated_merge(att, rec, w_att_o, w_rec_o, cols, ga_col, tm=1024, tn=1024):
    n, aw = att.shape
    rw = rec.shape[1]
    d = w_att_o.shape[1]
    nj = d // tn
    return pl.pallas_call(
        _merge_kernel,
        grid=(n // tm, nj),
        in_specs=[pl.BlockSpec((tm, aw), lambda i, j: (i, 0)),
                  pl.BlockSpec((tm, rw), lambda i, j: (i, 0)),
                  pl.BlockSpec((aw, tn), lambda i, j: (0, j)),
                  pl.BlockSpec((rw, tn), lambda i, j: (0, j)),
                  pl.BlockSpec((tm, tn), lambda i, j: (i, ga_col + j)),
                  pl.BlockSpec((tm, tn), lambda i, j: (i, ga_col + nj + j))],
        out_specs=pl.BlockSpec((tm, tn), lambda i, j: (i, j)),
        out_shape=jax.ShapeDtypeStruct((n, d), BF16),
        compiler_params=_cparams("arbitrary", "arbitrary"),
        name="gated_merge",
    )(att, rec, w_att_o, w_rec_o, cols, cols)


def _proj_res_norm_kernel(x_ref, w_ref, h_ref, g_ref, hout_ref, n_ref, *extra):
    hn = h_ref[...] + jnp.dot(x_ref[...], w_ref[...], preferred_element_type=F32)
    hout_ref[...] = hn
    nrm = _rms(hn) * g_ref[...]
    n_ref[...] = nrm.astype(n_ref.dtype)
    for r in extra:
        r[...] = nrm.astype(r.dtype)


def proj_res_norm(x, w, h, g, norm_dtypes, tm=256):
    n, k = x.shape
    d = w.shape[1]
    row = lambda: pl.BlockSpec((tm, d), lambda i: (i, 0))
    return pl.pallas_call(
        _proj_res_norm_kernel,
        grid=(n // tm,),
        in_specs=[pl.BlockSpec((tm, k), lambda i: (i, 0)),
                  _resident((k, d), lambda i: (0, 0)),
                  row(),
                  pl.BlockSpec((1, d), lambda i: (0, 0))],
        out_specs=[row()] + [row() for _ in norm_dtypes],
        out_shape=[jax.ShapeDtypeStruct((n, d), F32)] + [jax.ShapeDtypeStruct((n, d), dt) for dt in norm_dtypes],
        compiler_params=_cparams("arbitrary"),
        name="proj_res_norm",
    )(x, w, h, g.reshape(1, d))


def _ffn_up_kernel(x_ref, w1_ref, w3_ref, o_ref):
    x = x_ref[...]
    a = jnp.dot(x, w1_ref[...], preferred_element_type=F32)
    b = jnp.dot(x, w3_ref[...], preferred_element_type=F32)
    o_ref[...] = ((a * _sigmoid(a)) * b).astype(o_ref.dtype)


def ffn_up(xn, w1, w3, tm=1024, tn=512):
    n, d = xn.shape
    f = w1.shape[1]
    return pl.pallas_call(
        _ffn_up_kernel,
        grid=(n // tm, f // tn),
        in_specs=[pl.BlockSpec((tm, d), lambda i, j: (i, 0)),
                  pl.BlockSpec((d, tn), lambda i, j: (0, j)),
                  pl.BlockSpec((d, tn), lambda i, j: (0, j))],
        out_specs=pl.BlockSpec((tm, tn), lambda i, j: (i, j)),
        out_shape=jax.ShapeDtypeStruct((n, f), BF16),
        compiler_params=_cparams("arbitrary", "arbitrary"),
        name="ffn_up",
    )(xn, w1, w3)


ROUTE_E0, ROUTE_E1, ROUTE_G0, ROUTE_G1, ROUTE_R0, ROUTE_R1 = range(6)


def _router_kernel(x_ref, w_ref, tri_ref, route_ref, cnt_ref, carry):
    @pl.when(pl.program_id(0) == 0)
    def _():
        carry[...] = jnp.zeros_like(carry)

    tm = x_ref.shape[0]
    logits = jnp.dot(x_ref[...], w_ref[...], preferred_element_type=F32)
    lane = lax.broadcasted_iota(jnp.int32, (tm, LANES), 1).astype(F32)
    neg = np.float32(-np.inf)
    lg = jnp.where(lane < N_EXPERTS, logits, neg)
    m0 = jnp.max(lg, axis=-1, keepdims=True)
    e0 = jnp.min(jnp.where(lg == m0, lane, np.float32(LANES)), axis=-1, keepdims=True)
    lg1 = jnp.where(lane == e0, neg, lg)
    m1 = jnp.max(lg1, axis=-1, keepdims=True)
    e1 = jnp.min(jnp.where(lg1 == m1, lane, np.float32(LANES)), axis=-1, keepdims=True)
    ex = jnp.exp(m1 - m0)
    g0 = 1.0 / (1.0 + ex)
    g1 = ex / (1.0 + ex)
    sel0 = lane == e0
    sel1 = lane == e1
    member = jnp.where(sel0 | sel1, np.float32(1.0), np.float32(0.0))
    incl = jnp.dot(tri_ref[...], member.astype(BF16), preferred_element_type=F32)
    before = carry[...] + incl - member
    r0 = jnp.sum(jnp.where(sel0, before, 0.0), axis=-1, keepdims=True)
    r1 = jnp.sum(jnp.where(sel1, before, 0.0), axis=-1, keepdims=True)
    total = carry[...] + incl[tm - 1:tm, :]
    carry[...] = total
    cnt_ref[...] = jnp.broadcast_to(total, cnt_ref.shape)
    out = jnp.zeros((tm, LANES), F32)
    for k, v in ((ROUTE_E0, e0), (ROUTE_E1, e1), (ROUTE_G0, g0), (ROUTE_G1, g1), (ROUTE_R0, r0), (ROUTE_R1, r1)):
        out = jnp.where(lane == k, v, out)
    route_ref[...] = out


def router(hn, w_router, tm=512):
    n, d = hn.shape
    wr = jnp.zeros((d, LANES), BF16).at[:, :N_EXPERTS].set(w_router.astype(BF16))
    tri = jnp.asarray(np.tril(np.ones((tm, tm), np.float32)), BF16)
    return pl.pallas_call(
        _router_kernel,
        grid=(n // tm,),
        in_specs=[pl.BlockSpec((tm, d), lambda i: (i, 0)),
                  pl.BlockSpec((d, LANES), lambda i: (0, 0)),
                  pl.BlockSpec((tm, tm), lambda i: (0, 0))],
        out_specs=[pl.BlockSpec((tm, LANES), lambda i: (i, 0)),
                   pl.BlockSpec((SUBLANES, LANES), lambda i: (0, 0))],
        out_shape=[jax.ShapeDtypeStruct((n, LANES), F32),
                   jax.ShapeDtypeStruct((SUBLANES, LANES), F32)],
        scratch_shapes=[pltpu.VMEM((1, LANES), F32)],
        compiler_params=_cparams("arbitrary"),
        name="router",
    )(hn, wr, tri)


def _dispatch_kernel(dest_ref, zero_ref, x_ref, xs_ref, zbuf, sem, zsem, *, tm):
    step = pl.program_id(0)
    eb = EXPERT_BLOCK

    @pl.when(step == 0)
    def _():
        zbuf[...] = jnp.zeros_like(zbuf)
        for z in range(zero_ref.shape[0]):
            @pl.when(zero_ref[z] >= 0)
            def _():
                cp = pltpu.make_async_copy(zbuf, xs_ref.at[pl.ds(pl.multiple_of(zero_ref[z], eb), eb)], zsem)
                cp.start()
                cp.wait()

    base = step * (tm * TOP_K)

    def row_copy(r, d):
        return pltpu.make_async_copy(x_ref.at[pl.ds(r, 1)], xs_ref.at[pl.ds(d, 1)], sem)

    def issue(r, carry):
        row_copy(r, dest_ref[base + TOP_K * r]).start()
        row_copy(r, dest_ref[base + TOP_K * r + 1]).start()
        return carry

    lax.fori_loop(0, tm, issue, 0)
    for _ in range(TOP_K * tm):
        row_copy(0, 0).wait()


def dispatch(hn32, dest, zero_start, n_rows, tm=256):
    n, d = hn32.shape
    return pl.pallas_call(
        functools.partial(_dispatch_kernel, tm=tm),
        grid_spec=pltpu.PrefetchScalarGridSpec(
            num_scalar_prefetch=2,
            grid=(n // tm,),
            in_specs=[pl.BlockSpec((tm, d), lambda i, dest, tail: (i, 0))],
            out_specs=pl.BlockSpec(memory_space=pl.ANY),
            scratch_shapes=[pltpu.VMEM((EXPERT_BLOCK, d), F32),
                            pltpu.SemaphoreType.DMA,
                            pltpu.SemaphoreType.DMA]),
        out_shape=jax.ShapeDtypeStruct((n_rows, d), F32),
        compiler_params=_cparams("arbitrary"),
        name="moe_dispatch",
    )(dest, zero_start, hn32)


def _used_block(i, nu):
    return jnp.minimum(i, nu[0] - 1)


def _new_weights(be_ref, i):
    return (i == 0) | (be_ref[i] != be_ref[jnp.maximum(i - 1, 0)])


def _moe_up_kernel(be_ref, nu_ref, x_ref, w1_ref, w3_ref, o_ref, w1_bf, w3_bf):
    i = pl.program_id(1)
    used = i < nu_ref[0]

    @pl.when(used & _new_weights(be_ref, i))
    def _():
        w1_bf[...] = w1_ref[0, 0].astype(BF16)
        w3_bf[...] = w3_ref[0, 0].astype(BF16)

    @pl.when(used)
    def _():
        x = x_ref[...].astype(BF16)
        a = jnp.dot(x, w1_bf[...], preferred_element_type=F32)
        b = jnp.dot(x, w3_bf[...], preferred_element_type=F32)
        o_ref[...] = ((a * _sigmoid(a)) * b).astype(o_ref.dtype)

    @pl.when(jnp.logical_not(used))
    def _():
        o_ref[...] = jnp.zeros_like(o_ref)


def moe_up(xs, w1, w3, layer, block_e, n_used, tn=1024):
    rows, d = xs.shape
    f = w1.shape[3]
    tm = EXPERT_BLOCK
    w_spec = lambda: pl.BlockSpec((1, 1, d, tn), lambda j, i, be, nu: (layer, be[_used_block(i, nu)], 0, j))
    return pl.pallas_call(
        _moe_up_kernel,
        grid_spec=pltpu.PrefetchScalarGridSpec(
            num_scalar_prefetch=2,
            grid=(f // tn, rows // tm),
            in_specs=[pl.BlockSpec((tm, d), lambda j, i, be, nu: (_used_block(i, nu), 0)),
                      w_spec(), w_spec()],
            out_specs=pl.BlockSpec((tm, tn), lambda j, i, be, nu: (i, j)),
            scratch_shapes=[pltpu.VMEM((d, tn), BF16), pltpu.VMEM((d, tn), BF16)]),
        out_shape=jax.ShapeDtypeStruct((rows, f), BF16),
        compiler_params=_cparams("arbitrary", "arbitrary"),
        name="moe_up",
    )(block_e, n_used, xs, w1, w3)


def _moe_down_kernel(be_ref, nu_ref, a_ref, w2_ref, o_ref, w2_bf):
    i = pl.program_id(1)
    used = i < nu_ref[0]

    @pl.when(used & _new_weights(be_ref, i))
    def _():
        w2_bf[...] = w2_ref[0, 0].astype(BF16)

    @pl.when(used)
    def _():
        o_ref[...] = jnp.dot(a_ref[...], w2_bf[...], preferred_element_type=F32)

    @pl.when(jnp.logical_not(used))
    def _():
        o_ref[...] = jnp.zeros_like(o_ref)


def moe_down(act, w2, layer, block_e, n_used, tn=512):
    rows, f = act.shape
    d = w2.shape[3]
    tm = EXPERT_BLOCK
    return pl.pallas_call(
        _moe_down_kernel,
        grid_spec=pltpu.PrefetchScalarGridSpec(
            num_scalar_prefetch=2,
            grid=(d // tn, rows // tm),
            in_specs=[pl.BlockSpec((tm, f), lambda j, i, be, nu: (_used_block(i, nu), 0)),
                      pl.BlockSpec((1, 1, f, tn), lambda j, i, be, nu: (layer, be[_used_block(i, nu)], 0, j))],
            out_specs=pl.BlockSpec((tm, tn), lambda j, i, be, nu: (i, j)),
            scratch_shapes=[pltpu.VMEM((f, tn), BF16)]),
        out_shape=jax.ShapeDtypeStruct((rows, d), F32),
        compiler_params=_cparams("arbitrary", "arbitrary"),
        name="moe_down",
    )(block_e, n_used, act, w2)


def _combine_kernel(dest_ref, ys_ref, route_ref, h_ref, g_ref, hout_ref, n_ref, buf0, buf1, sems, *, tm):
    base = pl.program_id(0) * (tm * TOP_K)

    def row_copy(buf, slot, r, d):
        return pltpu.make_async_copy(ys_ref.at[pl.ds(d, 1)], buf.at[pl.ds(r, 1)], sems.at[slot])

    def issue(r, carry):
        row_copy(buf0, 0, r, dest_ref[base + TOP_K * r]).start()
        row_copy(buf1, 1, r, dest_ref[base + TOP_K * r + 1]).start()
        return carry

    lax.fori_loop(0, tm, issue, 0)
    for _ in range(tm):
        row_copy(buf0, 0, 0, 0).wait()
    for _ in range(tm):
        row_copy(buf1, 1, 0, 0).wait()

    route = route_ref[...]
    g0 = route[:, ROUTE_G0:ROUTE_G0 + 1]
    g1 = route[:, ROUTE_G1:ROUTE_G1 + 1]
    hn = h_ref[...] + (g0 * buf0[...] + g1 * buf1[...])
    hout_ref[...] = hn
    n_ref[...] = (_rms(hn) * g_ref[...]).astype(n_ref.dtype)


def combine(ys, dest, route, h, g, norm_dtype, tm=256):
    n, d = h.shape
    row = lambda: pl.BlockSpec((tm, d), lambda i, dest: (i, 0))
    return pl.pallas_call(
        functools.partial(_combine_kernel, tm=tm),
        grid_spec=pltpu.PrefetchScalarGridSpec(
            num_scalar_prefetch=1,
            grid=(n // tm,),
            in_specs=[pl.BlockSpec(memory_space=pl.ANY),
                      pl.BlockSpec((tm, LANES), lambda i, dest: (i, 0)),
                      row(),
                      pl.BlockSpec((1, d), lambda i, dest: (0, 0))],
            out_specs=[row(), row()],
            scratch_shapes=[pltpu.VMEM((tm, d), F32),
                            pltpu.VMEM((tm, d), F32),
                            pltpu.SemaphoreType.DMA((TOP_K,))]),
        out_shape=[jax.ShapeDtypeStruct((n, d), F32), jax.ShapeDtypeStruct((n, d), norm_dtype)],
        compiler_params=_cparams("arbitrary"),
        name="moe_combine",
    )(dest, ys, route, h, g.reshape(1, d))


def moe_layer(hn_bf, hn32, h, w_router, w1, w3, w2, layer, g_next, norm_dtype):
    n = h.shape[0]
    eb = EXPERT_BLOCK
    route, cnt = router(hn_bf, w_router)
    counts = cnt[0, :N_EXPERTS].astype(jnp.int32)
    padded = (counts + eb - 1) // eb * eb
    pend = jnp.cumsum(padded)
    pstart = pend - padded
    experts = route[:, ROUTE_E0:ROUTE_E1 + 1].astype(jnp.int32)
    ranks = route[:, ROUTE_R0:ROUTE_R1 + 1].astype(jnp.int32)
    dest = (pstart[experts] + ranks).reshape(-1)
    n_blocks = -(-(n * TOP_K) // eb) + N_EXPERTS
    block_e = jnp.minimum(jnp.searchsorted(pend // eb, jnp.arange(n_blocks), side="right"),
                          N_EXPERTS - 1).astype(jnp.int32)
    n_used = (pend[-1:] // eb).astype(jnp.int32)
    tail_start = jnp.where(padded > 0, pend - eb, -1)
    spare = (n_used[0] + jnp.arange(N_EXPERTS)) * eb
    spare_start = jnp.where(spare < n_blocks * eb, spare, -1)
    zero_start = jnp.concatenate([tail_start, spare_start]).astype(jnp.int32)
    xs = dispatch(hn32, dest, zero_start, n_blocks * eb)
    act = moe_up(xs, w1, w3, layer, block_e, n_used)
    ys = moe_down(act, w2, layer, block_e, n_used)
    return combine(ys, dest, route, h, g_next, norm_dtype)


def kernel(x, norm_mix_g, norm_ffn_g, norm_final_g, w_in, rel_bias, conv_w, conv_b, lru_wa, lru_ba, lru_wx, lru_bx, lru_lambda, w_att_o, w_rec_o, w_out, ffn_w1, ffn_w3, ffn_w2, router_w, moe_w1, moe_w3, moe_w2):
    b, s, d = x.shape
    n = b * s
    depth = w_in.shape[0]
    aw = N_HEADS * HEAD_DIM
    lw = conv_w.shape[2]
    assert lw == aw and d % aw == 0, "column-block indexing assumes equal branch widths"
    h = x.reshape(n, d)
    xn = rmsnorm(h, norm_mix_g[0], BF16)
    out = None
    for l in range(depth):
        cols = in_projection(xn, w_in[l].astype(BF16), aw, HEAD_DIM ** -0.5)
        att = chunked_attention(cols, _rel_bias_table(rel_bias[l]), s)
        w_gates = jnp.concatenate([lru_wa[l], lru_wx[l]], axis=-1).astype(BF16)
        rec = recurrent_branch(cols, conv_w[l], conv_b[l], w_gates, lru_ba[l], lru_bx[l], lru_lambda[l],
                               b, s, rx_col=3)
        merged = gated_merge(att, rec, w_att_o[l].astype(BF16), w_rec_o[l].astype(BF16), cols, ga_col=5)
        last = l == depth - 1
        g_next = norm_final_g if last else norm_mix_g[l + 1]
        next_dtype = F32 if last else BF16
        i = l // 2
        if l % 2 == 0:
            h, hn = proj_res_norm(merged, w_out[l].astype(BF16), h, norm_ffn_g[l], [BF16])
            act = ffn_up(hn, ffn_w1[i].astype(BF16), ffn_w3[i].astype(BF16))
            h, xn = proj_res_norm(act, ffn_w2[i].astype(BF16), h, g_next, [next_dtype])
        else:
            h, hn, hn32 = proj_res_norm(merged, w_out[l].astype(BF16), h, norm_ffn_g[l], [BF16, F32])
            h, xn = moe_layer(hn, hn32, h, router_w[i], moe_w1, moe_w3, moe_w2, i, g_next, next_dtype)
        out = xn
    return out.reshape(b, s, d)
```

```python
import functools

import numpy as np
import jax
import jax.numpy as jnp
from jax import lax
from jax.experimental import pallas as pl
from jax.experimental.pallas import tpu as pltpu

CHUNK = 64
LEFT_CHUNKS = 8
N_HEADS = 8
HEAD_DIM = 128
MAX_REL_DIST = 256
LRU_BLOCKS = 8
CONV_W = 4
LRU_C = 8.0
N_EXPERTS = 8
TOP_K = 2
EXPERT_BLOCK = 256
RMS_EPS = 1e-6
NEG_INF = -1e30

LANES = 128
SUBLANES = 8
VMEM_LIMIT_BYTES = 56 * 1024 * 1024

ATT_QBLOCK = 4 * CHUNK
ATT_KBLOCKS = (LEFT_CHUNKS * CHUNK) // ATT_QBLOCK + 1

ROW_DMA_UNROLL = 8

F32 = jnp.float32
BF16 = jnp.bfloat16


def _cparams(*sem):
    return pltpu.CompilerParams(dimension_semantics=sem, vmem_limit_bytes=VMEM_LIMIT_BYTES)


def _resident(shape, index_map):
    return pl.BlockSpec(shape, index_map, pipeline_mode=pl.Buffered(1))


def _rms(x):
    return x * lax.rsqrt(jnp.mean(x * x, axis=-1, keepdims=True) + RMS_EPS)


def _sigmoid(x):
    return 1.0 / (1.0 + jnp.exp(-x))


def _gelu_tanh(x):
    c = np.float32(np.sqrt(2.0 / np.pi))
    return 0.5 * x * (1.0 + jnp.tanh(c * (x + 0.044715 * (x * x * x))))


def _softplus(x):
    return jnp.maximum(x, 0.0) + jnp.log1p(jnp.exp(-jnp.abs(x)))


def _rmsnorm_kernel(x_ref, g_ref, o_ref):
    o_ref[...] = (_rms(x_ref[...]) * g_ref[...]).astype(o_ref.dtype)


def rmsnorm(x, g, out_dtype, tm=512):
    n, d = x.shape
    return pl.pallas_call(
        _rmsnorm_kernel,
        grid=(n // tm,),
        in_specs=[pl.BlockSpec((tm, d), lambda i: (i, 0)),
                  pl.BlockSpec((1, d), lambda i: (0, 0))],
        out_specs=pl.BlockSpec((tm, d), lambda i: (i, 0)),
        out_shape=jax.ShapeDtypeStruct((n, d), out_dtype),
        compiler_params=_cparams("arbitrary"),
        name="rmsnorm",
    )(x, g.reshape(1, d))


def _inproj_kernel(x_ref, w_ref, o_ref, *, q_scale):
    acc = jnp.dot(x_ref[...], w_ref[...], preferred_element_type=F32)
    s = jnp.where(pl.program_id(1) == 0, np.float32(q_scale), np.float32(1.0))
    o_ref[...] = (acc * s).astype(o_ref.dtype)


def in_projection(xn, w, q_width, q_scale, tm=1024):
    n, d = xn.shape
    width = w.shape[1]
    tn = q_width
    return pl.pallas_call(
        functools.partial(_inproj_kernel, q_scale=q_scale),
        grid=(n // tm, width // tn),
        in_specs=[pl.BlockSpec((tm, d), lambda i, j: (i, 0)),
                  pl.BlockSpec((d, tn), lambda i, j: (0, j))],
        out_specs=pl.BlockSpec((tm, tn), lambda i, j: (i, j)),
        out_shape=jax.ShapeDtypeStruct((n, width), BF16),
        compiler_params=_cparams("arbitrary", "arbitrary"),
        name="in_projection",
    )(xn, w)


def _rel_bias_table(rel_bias):
    qb = ATT_QBLOCK
    kw = ATT_KBLOCKS * ATT_QBLOCK
    n_heads, n_rel = rel_bias.shape
    period = qb + kw
    dist = LEFT_CHUNKS * CHUNK + (qb - 1) - np.arange(period)
    idx = np.clip(dist, -(CHUNK - 1), MAX_REL_DIST) + CHUNK - 1
    n_far = int(np.sum(idx == n_rel - 1)) - 1
    n_near = int(np.sum(idx == 0)) - 1
    assert np.array_equal(idx, np.concatenate([np.full(n_far, n_rel - 1), np.arange(n_rel)[::-1], np.full(n_near, 0)]))
    rb = rel_bias.astype(F32)
    diag = jnp.concatenate([jnp.broadcast_to(rb[:, -1:], (n_heads, n_far)), rb[:, ::-1],
                            jnp.broadcast_to(rb[:, :1], (n_heads, n_near))], axis=1)
    rep = jnp.broadcast_to(diag[:, None, :], (n_heads, qb, period)).reshape(n_heads, qb * period)
    skew = rep[:, :qb * (period - 1)].reshape(n_heads, qb, period - 1)
    tab = skew[:, :, qb - 1:qb - 1 + kw]
    qc = np.arange(qb)[:, None] // CHUNK
    kc = np.arange(kw)[None, :] // CHUNK
    valid = (kc >= qc) & (kc <= qc + LEFT_CHUNKS)
    return jnp.where(valid[None], tab, NEG_INF)


def _attn_kernel(q_ref, k0_ref, k1_ref, k2_ref, v0_ref, v1_ref, v2_ref, b_ref, o_ref, *, blocks_per_seq):
    i = pl.program_id(0) % blocks_per_seq
    pen0 = jnp.where(i >= 2, np.float32(0.0), np.float32(NEG_INF))
    pen1 = jnp.where(i >= 1, np.float32(0.0), np.float32(NEG_INF))
    qb = ATT_QBLOCK
    nt = (((1,), (1,)), ((), ()))
    for h in range(N_HEADS):
        sl = slice(h * HEAD_DIM, (h + 1) * HEAD_DIM)
        q = q_ref[:, sl]
        s0 = lax.dot_general(q, k0_ref[:, sl], nt, preferred_element_type=F32) + b_ref[h, :, 0:qb] + pen0
        s1 = lax.dot_general(q, k1_ref[:, sl], nt, preferred_element_type=F32) + b_ref[h, :, qb:2 * qb] + pen1
        s2 = lax.dot_general(q, k2_ref[:, sl], nt, preferred_element_type=F32) + b_ref[h, :, 2 * qb:3 * qb]
        m = jnp.max(jnp.maximum(jnp.maximum(s0, s1), s2), axis=-1, keepdims=True)
        p0 = jnp.exp(s0 - m)
        p1 = jnp.exp(s1 - m)
        p2 = jnp.exp(s2 - m)
        l = jnp.sum(p0 + p1 + p2, axis=-1, keepdims=True)
        acc = jnp.dot(p0.astype(BF16), v0_ref[:, sl], preferred_element_type=F32)
        acc = acc + jnp.dot(p1.astype(BF16), v1_ref[:, sl], preferred_element_type=F32)
        acc = acc + jnp.dot(p2.astype(BF16), v2_ref[:, sl], preferred_element_type=F32)
        o_ref[:, sl] = (acc / l).astype(o_ref.dtype)


def chunked_attention(cols, bias_tab, seq):
    n = cols.shape[0]
    qb = ATT_QBLOCK
    aw = N_HEADS * HEAD_DIM
    bps = seq // qb
    assert ATT_KBLOCKS == 3 and seq % qb == 0

    def kv_spec(col, back):
        return pl.BlockSpec((qb, aw), lambda g: (g - jnp.minimum(g % bps, back), col))

    return pl.pallas_call(
        functools.partial(_attn_kernel, blocks_per_seq=bps),
        grid=(n // qb,),
        in_specs=[pl.BlockSpec((qb, aw), lambda g: (g, 0)),
                  kv_spec(1, 2), kv_spec(1, 1), kv_spec(1, 0),
                  kv_spec(2, 2), kv_spec(2, 1), kv_spec(2, 0),
                  _resident(bias_tab.shape, lambda g: (0, 0, 0))],
        out_specs=pl.BlockSpec((qb, aw), lambda g: (g, 0)),
        out_shape=jax.ShapeDtypeStruct((n, aw), BF16),
        compiler_params=_cparams("arbitrary"),
        name="chunked_attention",
    )(cols, cols, cols, cols, cols, cols, cols, bias_tab)


def _rec_kernel(rx_ref, ry_ref, cw_ref, cb_ref, wg_ref, ba_ref, bx_ref, lam_ref, o_ref,
                xbuf, a_s, u_s, carry, *, rows, width, bw):
    t = pl.program_id(1)
    pad = SUBLANES

    @pl.when(t == 0)
    def _():
        xbuf[0:pad, :] = jnp.zeros((pad, width), F32)
        carry[...] = jnp.zeros((pad, width), F32)

    x = rx_ref[...].astype(F32)
    xbuf[pad:pad + rows, :] = x
    rc = cb_ref[...] + cw_ref[0:1, :] * xbuf[pad - 3:pad - 3 + rows, :]
    rc = rc + cw_ref[1:2, :] * xbuf[pad - 2:pad - 2 + rows, :]
    rc = rc + cw_ref[2:3, :] * xbuf[pad - 1:pad - 1 + rows, :]
    rc = rc + cw_ref[3:4, :] * x
    xbuf[0:pad, :] = xbuf[rows:rows + pad, :]
    u_s[...] = rc

    neg_c_sp = -LRU_C * _softplus(-lam_ref[...])
    for nb in range(width // bw):
        sl = slice(nb * bw, (nb + 1) * bw)
        rcn = u_s[:, sl]
        g = jnp.dot(rcn.astype(BF16), wg_ref[nb], preferred_element_type=F32)
        r = _sigmoid(g[:, :bw] + ba_ref[:, sl])
        ig = _sigmoid(g[:, bw:] + bx_ref[:, sl])
        log_a = neg_c_sp[:, sl] * r
        a_s[:, sl] = jnp.exp(log_a)
        th = jnp.tanh(log_a)
        u_s[:, sl] = jnp.sqrt(-2.0 * th / (1.0 - th)) * (ig * rcn)

    row = lax.broadcasted_iota(jnp.int32, (pad, width), 0)

    def group(gi, c):
        r0 = pl.multiple_of(gi * pad, pad)
        a = a_s[pl.ds(r0, pad), :]
        h = u_s[pl.ds(r0, pad), :]
        for s in (1, 2, 4):
            keep = row >= s
            a_prev = jnp.where(keep, pltpu.roll(a, s, 0), 1.0)
            h_prev = jnp.where(keep, pltpu.roll(h, s, 0), 0.0)
            h = a * h_prev + h
            a = a * a_prev
        h = h + a * c
        u_s[pl.ds(r0, pad), :] = h
        return jnp.broadcast_to(h[pad - 1:pad, :], (pad, width))

    carry[...] = lax.fori_loop(0, rows // pad, group, carry[...])
    o_ref[...] = (_gelu_tanh(ry_ref[...].astype(F32)) * u_s[...]).astype(o_ref.dtype)


def recurrent_branch(cols, conv_w, conv_b, w_gates, b_a, b_x, lam, batch, seq, rx_col, rows=256):
    n = cols.shape[0]
    width = conv_w.shape[1]
    bw = width // LRU_BLOCKS
    spb = seq // rows
    vec = lambda: pl.BlockSpec((1, width), lambda b, t: (0, 0))
    return pl.pallas_call(
        functools.partial(_rec_kernel, rows=rows, width=width, bw=bw),
        grid=(batch, spb),
        in_specs=[pl.BlockSpec((rows, width), lambda b, t: (b * spb + t, rx_col)),
                  pl.BlockSpec((rows, width), lambda b, t: (b * spb + t, rx_col + 1)),
                  pl.BlockSpec((CONV_W, width), lambda b, t: (0, 0)),
                  vec(),
                  pl.BlockSpec((LRU_BLOCKS, bw, 2 * bw), lambda b, t: (0, 0, 0)),
                  vec(), vec(), vec()],
        out_specs=pl.BlockSpec((rows, width), lambda b, t: (b * spb + t, 0)),
        out_shape=jax.ShapeDtypeStruct((n, width), BF16),
        scratch_shapes=[pltpu.VMEM((rows + SUBLANES, width), F32),
                        pltpu.VMEM((rows, width), F32),
                        pltpu.VMEM((rows, width), F32),
                        pltpu.VMEM((SUBLANES, width), F32)],
        compiler_params=_cparams("arbitrary", "arbitrary"),
        name="recurrent_branch",
    )(cols, cols, conv_w, conv_b.reshape(1, width), w_gates,
      b_a.reshape(1, width), b_x.reshape(1, width), lam.reshape(1, width))


def _merge_kernel(att_ref, rec_ref, wa_ref, wr_ref, ga_ref, gb_ref, o_ref):
    ya = jnp.dot(att_ref[...], wa_ref[...], preferred_element_type=F32)
    yr = jnp.dot(rec_ref[...], wr_ref[...], preferred_element_type=F32)
    o = _sigmoid(ga_ref[...].astype(F32)) * ya + _sigmoid(gb_ref[...].astype(F32)) * yr
    o_ref[...] = o.astype(o_ref.dtype)


def gated_merge(att, rec, w_att_o, w_rec_o, cols, ga_col, tm=1024, tn=1024):
    n, aw = att.shape
    rw = rec.shape[1]
    d = w_att_o.shape[1]
    nj = d // tn
    return pl.pallas_call(
        _merge_kernel,
        grid=(n // tm, nj),
        in_specs=[pl.BlockSpec((tm, aw), lambda i, j: (i, 0)),
                  pl.BlockSpec((tm, rw), lambda i, j: (i, 0)),
                  pl.BlockSpec((aw, tn), lambda i, j: (0, j)),
                  pl.BlockSpec((rw, tn), lambda i, j: (0, j)),
                  pl.BlockSpec((tm, tn), lambda i, j: (i, ga_col + j)),
                  pl.BlockSpec((tm, tn), lambda i, j: (i, ga_col + nj + j))],
        out_specs=pl.BlockSpec((tm, tn), lambda i, j: (i, j)),
        out_shape=jax.ShapeDtypeStruct((n, d), BF16),
        compiler_params=_cparams("arbitrary", "arbitrary"),
        name="gated_merge",
    )(att, rec, w_att_o, w_rec_o, cols, cols)


def _proj_res_norm_kernel(x_ref, w_ref, h_ref, g_ref, hout_ref, n_ref, *extra):
    hn = h_ref[...] + jnp.dot(x_ref[...], w_ref[...], preferred_element_type=F32)
    hout_ref[...] = hn
    nrm = _rms(hn) * g_ref[...]
    n_ref[...] = nrm.astype(n_ref.dtype)
    for r in extra:
        r[...] = nrm.astype(r.dtype)


def proj_res_norm(x, w, h, g, norm_dtypes, tm=256):
    n, k = x.shape
    d = w.shape[1]
    row = lambda: pl.BlockSpec((tm, d), lambda i: (i, 0))
    return pl.pallas_call(
        _proj_res_norm_kernel,
        grid=(n // tm,),
        in_specs=[pl.BlockSpec((tm, k), lambda i: (i, 0)),
                  _resident((k, d), lambda i: (0, 0)),
                  row(),
                  pl.BlockSpec((1, d), lambda i: (0, 0))],
        out_specs=[row()] + [row() for _ in norm_dtypes],
        out_shape=[jax.ShapeDtypeStruct((n, d), F32)] + [jax.ShapeDtypeStruct((n, d), dt) for dt in norm_dtypes],
        compiler_params=_cparams("arbitrary"),
        name="proj_res_norm",
    )(x, w, h, g.reshape(1, d))


def _ffn_up_kernel(x_ref, w1_ref, w3_ref, o_ref):
    x = x_ref[...]
    a = jnp.dot(x, w1_ref[...], preferred_element_type=F32)
    b = jnp.dot(x, w3_ref[...], preferred_element_type=F32)
    o_ref[...] = ((a * _sigmoid(a)) * b).astype(o_ref.dtype)


def ffn_up(xn, w1, w3, tm=1024, tn=512):
    n, d = xn.shape
    f = w1.shape[1]
    return pl.pallas_call(
        _ffn_up_kernel,
        grid=(n // tm, f // tn),
        in_specs=[pl.BlockSpec((tm, d), lambda i, j: (i, 0)),
                  pl.BlockSpec((d, tn), lambda i, j: (0, j)),
                  pl.BlockSpec((d, tn), lambda i, j: (0, j))],
        out_specs=pl.BlockSpec((tm, tn), lambda i, j: (i, j)),
        out_shape=jax.ShapeDtypeStruct((n, f), BF16),
        compiler_params=_cparams("arbitrary", "arbitrary"),
        name="ffn_up",
    )(xn, w1, w3)


ROUTE_E0, ROUTE_E1, ROUTE_G0, ROUTE_G1, ROUTE_R0, ROUTE_R1 = range(6)


def _router_kernel(x_ref, w_ref, tri_ref, route_ref, cnt_ref, carry):
    @pl.when(pl.program_id(0) == 0)
    def _():
        carry[...] = jnp.zeros_like(carry)

    tm = x_ref.shape[0]
    logits = jnp.dot(x_ref[...], w_ref[...], preferred_element_type=F32)
    lane = lax.broadcasted_iota(jnp.int32, (tm, LANES), 1).astype(F32)
    neg = np.float32(-np.inf)
    lg = jnp.where(lane < N_EXPERTS, logits, neg)
    m0 = jnp.max(lg, axis=-1, keepdims=True)
    e0 = jnp.min(jnp.where(lg == m0, lane, np.float32(LANES)), axis=-1, keepdims=True)
    lg1 = jnp.where(lane == e0, neg, lg)
    m1 = jnp.max(lg1, axis=-1, keepdims=True)
    e1 = jnp.min(jnp.where(lg1 == m1, lane, np.float32(LANES)), axis=-1, keepdims=True)
    ex = jnp.exp(m1 - m0)
    g0 = 1.0 / (1.0 + ex)
    g1 = ex / (1.0 + ex)
    sel0 = lane == e0
    sel1 = lane == e1
    member = jnp.where(sel0 | sel1, np.float32(1.0), np.float32(0.0))
    incl = jnp.dot(tri_ref[...], member.astype(BF16), preferred_element_type=F32)
    before = carry[...] + incl - member
    r0 = jnp.sum(jnp.where(sel0, before, 0.0), axis=-1, keepdims=True)
    r1 = jnp.sum(jnp.where(sel1, before, 0.0), axis=-1, keepdims=True)
    total = carry[...] + incl[tm - 1:tm, :]
    carry[...] = total
    cnt_ref[...] = jnp.broadcast_to(total, cnt_ref.shape)
    out = jnp.zeros((tm, LANES), F32)
    for k, v in ((ROUTE_E0, e0), (ROUTE_E1, e1), (ROUTE_G0, g0), (ROUTE_G1, g1), (ROUTE_R0, r0), (ROUTE_R1, r1)):
        out = jnp.where(lane == k, v, out)
    route_ref[...] = out


def router(hn, w_router, tm=512):
    n, d = hn.shape
    wr = jnp.zeros((d, LANES), BF16).at[:, :N_EXPERTS].set(w_router.astype(BF16))
    tri = jnp.asarray(np.tril(np.ones((tm, tm), np.float32)), BF16)
    return pl.pallas_call(
        _router_kernel,
        grid=(n // tm,),
        in_specs=[pl.BlockSpec((tm, d), lambda i: (i, 0)),
                  pl.BlockSpec((d, LANES), lambda i: (0, 0)),
                  pl.BlockSpec((tm, tm), lambda i: (0, 0))],
        out_specs=[pl.BlockSpec((tm, LANES), lambda i: (i, 0)),
                   pl.BlockSpec((SUBLANES, LANES), lambda i: (0, 0))],
        out_shape=[jax.ShapeDtypeStruct((n, LANES), F32),
                   jax.ShapeDtypeStruct((SUBLANES, LANES), F32)],
        scratch_shapes=[pltpu.VMEM((1, LANES), F32)],
        compiler_params=_cparams("arbitrary"),
        name="router",
    )(hn, wr, tri)


def _dispatch_kernel(dest_ref, zero_ref, x_ref, xs_ref, zbuf, sem, zsem, *, tm):
    step = pl.program_id(0)
    eb = EXPERT_BLOCK

    @pl.when(step == 0)
    def _():
        zbuf[...] = jnp.zeros_like(zbuf)
        for z in range(zero_ref.shape[0]):
            @pl.when(zero_ref[z] >= 0)
            def _():
                cp = pltpu.make_async_copy(zbuf, xs_ref.at[pl.ds(pl.multiple_of(zero_ref[z], eb), eb)], zsem)
                cp.start()
                cp.wait()

    base = step * (tm * TOP_K)

    def row_copy(r, d):
        return pltpu.make_async_copy(x_ref.at[pl.ds(r, 1)], xs_ref.at[pl.ds(d, 1)], sem)

    def issue(r, carry):
        row_copy(r, dest_ref[base + TOP_K * r]).start()
        row_copy(r, dest_ref[base + TOP_K * r + 1]).start()
        return carry

    lax.fori_loop(0, tm, issue, 0, unroll=ROW_DMA_UNROLL)
    for _ in range(TOP_K * tm):
        row_copy(0, 0).wait()


def dispatch(hn32, dest, zero_start, n_rows, tm=256):
    n, d = hn32.shape
    return pl.pallas_call(
        functools.partial(_dispatch_kernel, tm=tm),
        grid_spec=pltpu.PrefetchScalarGridSpec(
            num_scalar_prefetch=2,
            grid=(n // tm,),
            in_specs=[pl.BlockSpec((tm, d), lambda i, dest, tail: (i, 0))],
            out_specs=pl.BlockSpec(memory_space=pl.ANY),
            scratch_shapes=[pltpu.VMEM((EXPERT_BLOCK, d), F32),
                            pltpu.SemaphoreType.DMA,
                            pltpu.SemaphoreType.DMA]),
        out_shape=jax.ShapeDtypeStruct((n_rows, d), F32),
        compiler_params=_cparams("arbitrary"),
        name="moe_dispatch",
    )(dest, zero_start, hn32)


def _grouped_rows_kernel(start_ref, count_ref, spare_ref, x_hbm, *refs, n_weights, tm, tn, tile_fn):
    w_refs = refs[:n_weights]
    o_hbm = refs[n_weights]
    w_bf = refs[n_weights + 1:2 * n_weights + 1]
    xbuf, obuf, zbuf, xsem, osem, zsem = refs[2 * n_weights + 1:]
    j = pl.program_id(0)
    e = pl.program_id(1)
    nb = count_ref[e]
    first = start_ref[e]
    col0 = pl.multiple_of(j * tn, tn)

    def rows_of(blk):
        return pl.ds(pl.multiple_of((first + blk) * tm, tm), tm)

    def x_copy(blk, slot):
        return pltpu.make_async_copy(x_hbm.at[rows_of(blk)], xbuf.at[slot], xsem.at[slot])

    def o_copy(blk, slot):
        return pltpu.make_async_copy(obuf.at[slot], o_hbm.at[rows_of(blk), pl.ds(col0, tn)], osem.at[slot])

    @pl.when(nb > 0)
    def _():
        x_copy(0, 0).start()
        for wb, w in zip(w_bf, w_refs):
            wb[...] = w[0, 0].astype(BF16)

        def block(i, carry):
            slot = lax.rem(i, 2)
            x_copy(i, slot).wait()

            @pl.when(i + 1 < nb)
            def _():
                x_copy(i + 1, 1 - slot).start()

            @pl.when(i >= 2)
            def _():
                o_copy(i - 2, slot).wait()

            obuf[slot] = tile_fn(xbuf[slot], *w_bf)
            o_copy(i, slot).start()
            return carry

        lax.fori_loop(0, nb, block, 0)

        @pl.when(nb >= 2)
        def _():
            o_copy(nb - 2, lax.rem(nb, 2)).wait()

        o_copy(nb - 1, lax.rem(nb - 1, 2)).wait()

    @pl.when(e == pl.num_programs(1) - 1)
    def _():
        zbuf[...] = jnp.zeros_like(zbuf)

        def z_copy(z):
            rows = pl.ds(pl.multiple_of(spare_ref[z], tm), tm)
            return pltpu.make_async_copy(zbuf, o_hbm.at[rows, pl.ds(col0, tn)], zsem)

        for z in range(spare_ref.shape[0]):
            @pl.when(spare_ref[z] >= 0)
            def _():
                z_copy(z).start()
        for z in range(spare_ref.shape[0]):
            @pl.when(spare_ref[z] >= 0)
            def _():
                z_copy(z).wait()


def _grouped_rows(x, weights, layer, groups, out_dtype, tn, tile_fn, name):
    start_blk, count_blk, spare_start = groups
    rows, k = x.shape
    n_exp = weights[0].shape[1]
    width = weights[0].shape[3]
    tm = EXPERT_BLOCK
    w_spec = lambda: pl.BlockSpec((1, 1, k, tn), lambda j, e, *_: (layer, e, 0, j))
    return pl.pallas_call(
        functools.partial(_grouped_rows_kernel, n_weights=len(weights), tm=tm, tn=tn, tile_fn=tile_fn),
        grid_spec=pltpu.PrefetchScalarGridSpec(
            num_scalar_prefetch=3,
            grid=(width // tn, n_exp),
            in_specs=[pl.BlockSpec(memory_space=pl.ANY)] + [w_spec() for _ in weights],
            out_specs=pl.BlockSpec(memory_space=pl.ANY),
            scratch_shapes=[pltpu.VMEM((k, tn), BF16) for _ in weights] + [
                pltpu.VMEM((2, tm, k), x.dtype),
                pltpu.VMEM((2, tm, tn), out_dtype),
                pltpu.VMEM((tm, tn), out_dtype),
                pltpu.SemaphoreType.DMA((2,)),
                pltpu.SemaphoreType.DMA((2,)),
                pltpu.SemaphoreType.DMA]),
        out_shape=jax.ShapeDtypeStruct((rows, width), out_dtype),
        compiler_params=_cparams("arbitrary", "arbitrary"),
        name=name,
    )(start_blk, count_blk, spare_start, x, *weights)


def _swiglu_tile(x, w1_bf, w3_bf):
    xb = x.astype(BF16)
    a = jnp.dot(xb, w1_bf[...], preferred_element_type=F32)
    b = jnp.dot(xb, w3_bf[...], preferred_element_type=F32)
    return ((a * _sigmoid(a)) * b).astype(BF16)


def _matmul_tile(x, w_bf):
    return jnp.dot(x, w_bf[...], preferred_element_type=F32)


def moe_up(xs, w1, w3, layer, groups, tn=1024):
    return _grouped_rows(xs, (w1, w3), layer, groups, BF16, tn, _swiglu_tile, "moe_up")


def moe_down(act, w2, layer, groups, tn=512):
    return _grouped_rows(act, (w2,), layer, groups, F32, tn, _matmul_tile, "moe_down")


def _combine_kernel(dest_ref, ys_ref, route_ref, h_ref, g_ref, hout_ref, n_ref, buf0, buf1, sems, *, tm):
    base = pl.program_id(0) * (tm * TOP_K)

    def row_copy(buf, slot, r, d):
        return pltpu.make_async_copy(ys_ref.at[pl.ds(d, 1)], buf.at[pl.ds(r, 1)], sems.at[slot])

    def issue(r, carry):
        row_copy(buf0, 0, r, dest_ref[base + TOP_K * r]).start()
        row_copy(buf1, 1, r, dest_ref[base + TOP_K * r + 1]).start()
        return carry

    lax.fori_loop(0, tm, issue, 0, unroll=ROW_DMA_UNROLL)
    for _ in range(tm):
        row_copy(buf0, 0, 0, 0).wait()
    for _ in range(tm):
        row_copy(buf1, 1, 0, 0).wait()

    route = route_ref[...]
    g0 = route[:, ROUTE_G0:ROUTE_G0 + 1]
    g1 = route[:, ROUTE_G1:ROUTE_G1 + 1]
    hn = h_ref[...] + (g0 * buf0[...] + g1 * buf1[...])
    hout_ref[...] = hn
    n_ref[...] = (_rms(hn) * g_ref[...]).astype(n_ref.dtype)


def combine(ys, dest, route, h, g, norm_dtype, tm=256):
    n, d = h.shape
    row = lambda: pl.BlockSpec((tm, d), lambda i, dest: (i, 0))
    return pl.pallas_call(
        functools.partial(_combine_kernel, tm=tm),
        grid_spec=pltpu.PrefetchScalarGridSpec(
            num_scalar_prefetch=1,
            grid=(n // tm,),
            in_specs=[pl.BlockSpec(memory_space=pl.ANY),
                      pl.BlockSpec((tm, LANES), lambda i, dest: (i, 0)),
                      row(),
                      pl.BlockSpec((1, d), lambda i, dest: (0, 0))],
            out_specs=[row(), row()],
            scratch_shapes=[pltpu.VMEM((tm, d), F32),
                            pltpu.VMEM((tm, d), F32),
                            pltpu.SemaphoreType.DMA((TOP_K,))]),
        out_shape=[jax.ShapeDtypeStruct((n, d), F32), jax.ShapeDtypeStruct((n, d), norm_dtype)],
        compiler_params=_cparams("arbitrary"),
        name="moe_combine",
    )(dest, ys, route, h, g.reshape(1, d))


def moe_layer(hn_bf, hn32, h, w_router, w1, w3, w2, layer, g_next, norm_dtype):
    n = h.shape[0]
    eb = EXPERT_BLOCK
    route, cnt = router(hn_bf, w_router)
    counts = cnt[0, :N_EXPERTS].astype(jnp.int32)
    padded = (counts + eb - 1) // eb * eb
    pend = jnp.cumsum(padded)
    pstart = pend - padded
    experts = route[:, ROUTE_E0:ROUTE_E1 + 1].astype(jnp.int32)
    ranks = route[:, ROUTE_R0:ROUTE_R1 + 1].astype(jnp.int32)
    dest = (pstart[experts] + ranks).reshape(-1)
    n_blocks = -(-(n * TOP_K) // eb) + N_EXPERTS
    tail_start = jnp.where(padded > 0, pend - eb, -1)
    spare = pend[-1] + jnp.arange(N_EXPERTS) * eb
    spare_start = jnp.where(spare < n_blocks * eb, spare, -1).astype(jnp.int32)
    zero_start = jnp.concatenate([tail_start.astype(jnp.int32), spare_start])
    groups = ((pstart // eb).astype(jnp.int32), (padded // eb).astype(jnp.int32), spare_start)
    xs = dispatch(hn32, dest, zero_start, n_blocks * eb)
    act = moe_up(xs, w1, w3, layer, groups)
    ys = moe_down(act, w2, layer, groups)
    return combine(ys, dest, route, h, g_next, norm_dtype)


def kernel(x, norm_mix_g, norm_ffn_g, norm_final_g, w_in, rel_bias, conv_w, conv_b, lru_wa, lru_ba, lru_wx, lru_bx, lru_lambda, w_att_o, w_rec_o, w_out, ffn_w1, ffn_w3, ffn_w2, router_w, moe_w1, moe_w3, moe_w2):
    b, s, d = x.shape
    n = b * s
    depth = w_in.shape[0]
    aw = N_HEADS * HEAD_DIM
    lw = conv_w.shape[2]
    assert lw == aw and d % aw == 0, "column-block indexing assumes equal branch widths"
    h = x.reshape(n, d)
    xn = rmsnorm(h, norm_mix_g[0], BF16)
    out = None
    for l in range(depth):
        cols = in_projection(xn, w_in[l].astype(BF16), aw, HEAD_DIM ** -0.5)
        att = chunked_attention(cols, _rel_bias_table(rel_bias[l]), s)
        w_gates = jnp.concatenate([lru_wa[l], lru_wx[l]], axis=-1).astype(BF16)
        rec = recurrent_branch(cols, conv_w[l], conv_b[l], w_gates, lru_ba[l], lru_bx[l], lru_lambda[l],
                               b, s, rx_col=3)
        merged = gated_merge(att, rec, w_att_o[l].astype(BF16), w_rec_o[l].astype(BF16), cols, ga_col=5)
        last = l == depth - 1
        g_next = norm_final_g if last else norm_mix_g[l + 1]
        next_dtype = F32 if last else BF16
        i = l // 2
        if l % 2 == 0:
            h, hn = proj_res_norm(merged, w_out[l].astype(BF16), h, norm_ffn_g[l], [BF16])
            act = ffn_up(hn, ffn_w1[i].astype(BF16), ffn_w3[i].astype(BF16))
            h, xn = proj_res_norm(act, ffn_w2[i].astype(BF16), h, g_next, [next_dtype])
        else:
            h, hn, hn32 = proj_res_norm(merged, w_out[l].astype(BF16), h, norm_ffn_g[l], [BF16, F32])
            h, xn = moe_layer(hn, hn32, h, router_w[i], moe_w1, moe_w3, moe_w2, i, g_next, next_dtype)
        out = xn
    return out.reshape(b, s, d)
```

```python
import functools

import numpy as np
import jax
import jax.numpy as jnp
from jax import lax
from jax.experimental import pallas as pl
from jax.experimental.pallas import tpu as pltpu

CHUNK = 64
LEFT_CHUNKS = 8
N_HEADS = 8
HEAD_DIM = 128
MAX_REL_DIST = 256
LRU_BLOCKS = 8
CONV_W = 4
LRU_C = 8.0
N_EXPERTS = 8
TOP_K = 2
EXPERT_BLOCK = 256
RMS_EPS = 1e-6
NEG_INF = -1e30

LANES = 128
SUBLANES = 8
VMEM_LIMIT_BYTES = 56 * 1024 * 1024

ATT_QBLOCK = 4 * CHUNK
ATT_KBLOCKS = (LEFT_CHUNKS * CHUNK) // ATT_QBLOCK + 1

ROW_DMA_UNROLL = 8

GROUPED_IN_SLOTS = 3
GROUPED_OUT_SLOTS = 2
MAX_PLAIN_DMA_BYTES = 2 * 1024 * 1024

F32 = jnp.float32
BF16 = jnp.bfloat16


def _cparams(*sem):
    return pltpu.CompilerParams(dimension_semantics=sem, vmem_limit_bytes=VMEM_LIMIT_BYTES)


def _resident(shape, index_map):
    return pl.BlockSpec(shape, index_map, pipeline_mode=pl.Buffered(1))


def _rms(x):
    return x * lax.rsqrt(jnp.mean(x * x, axis=-1, keepdims=True) + RMS_EPS)


def _sigmoid(x):
    return 1.0 / (1.0 + jnp.exp(-x))


def _gelu_tanh(x):
    c = np.float32(np.sqrt(2.0 / np.pi))
    return 0.5 * x * (1.0 + jnp.tanh(c * (x + 0.044715 * (x * x * x))))


def _softplus(x):
    return jnp.maximum(x, 0.0) + jnp.log1p(jnp.exp(-jnp.abs(x)))


def _rmsnorm_kernel(x_ref, g_ref, o_ref):
    o_ref[...] = (_rms(x_ref[...]) * g_ref[...]).astype(o_ref.dtype)


def rmsnorm(x, g, out_dtype, tm=512):
    n, d = x.shape
    return pl.pallas_call(
        _rmsnorm_kernel,
        grid=(n // tm,),
        in_specs=[pl.BlockSpec((tm, d), lambda i: (i, 0)),
                  pl.BlockSpec((1, d), lambda i: (0, 0))],
        out_specs=pl.BlockSpec((tm, d), lambda i: (i, 0)),
        out_shape=jax.ShapeDtypeStruct((n, d), out_dtype),
        compiler_params=_cparams("arbitrary"),
        name="rmsnorm",
    )(x, g.reshape(1, d))


def _inproj_kernel(x_ref, w_ref, o_ref, w_bf, *, q_scale):
    @pl.when(pl.program_id(1) == 0)
    def _():
        w_bf[...] = w_ref[0].astype(BF16)

    acc = jnp.dot(x_ref[...], w_bf[...], preferred_element_type=F32)
    s = jnp.where(pl.program_id(0) == 0, np.float32(q_scale), np.float32(1.0))
    o_ref[...] = (acc * s).astype(o_ref.dtype)


def in_projection(xn, w, layer, q_width, q_scale, tm=1024):
    n, d = xn.shape
    width = w.shape[2]
    tn = q_width
    return pl.pallas_call(
        functools.partial(_inproj_kernel, q_scale=q_scale),
        grid=(width // tn, n // tm),
        in_specs=[pl.BlockSpec((tm, d), lambda j, i: (i, 0)),
                  pl.BlockSpec((1, d, tn), lambda j, i: (layer, 0, j))],
        out_specs=pl.BlockSpec((tm, tn), lambda j, i: (i, j)),
        out_shape=jax.ShapeDtypeStruct((n, width), BF16),
        scratch_shapes=[pltpu.VMEM((d, tn), BF16)],
        compiler_params=_cparams("arbitrary", "arbitrary"),
        name="in_projection",
    )(xn, w)


def _rel_bias_table(rel_bias):
    qb = ATT_QBLOCK
    kw = ATT_KBLOCKS * ATT_QBLOCK
    n_heads, n_rel = rel_bias.shape
    period = qb + kw
    dist = LEFT_CHUNKS * CHUNK + (qb - 1) - np.arange(period)
    idx = np.clip(dist, -(CHUNK - 1), MAX_REL_DIST) + CHUNK - 1
    n_far = int(np.sum(idx == n_rel - 1)) - 1
    n_near = int(np.sum(idx == 0)) - 1
    assert np.array_equal(idx, np.concatenate([np.full(n_far, n_rel - 1), np.arange(n_rel)[::-1], np.full(n_near, 0)]))
    rb = rel_bias.astype(F32)
    diag = jnp.concatenate([jnp.broadcast_to(rb[:, -1:], (n_heads, n_far)), rb[:, ::-1],
                            jnp.broadcast_to(rb[:, :1], (n_heads, n_near))], axis=1)
    rep = jnp.broadcast_to(diag[:, None, :], (n_heads, qb, period)).reshape(n_heads, qb * period)
    skew = rep[:, :qb * (period - 1)].reshape(n_heads, qb, period - 1)
    tab = skew[:, :, qb - 1:qb - 1 + kw]
    qc = np.arange(qb)[:, None] // CHUNK
    kc = np.arange(kw)[None, :] // CHUNK
    valid = (kc >= qc) & (kc <= qc + LEFT_CHUNKS)
    return jnp.where(valid[None], tab, NEG_INF)


def _attn_kernel(q_ref, k0_ref, k1_ref, k2_ref, v0_ref, v1_ref, v2_ref, b_ref, o_ref, *, blocks_per_seq):
    i = pl.program_id(0) % blocks_per_seq
    pen0 = jnp.where(i >= 2, np.float32(0.0), np.float32(NEG_INF))
    pen1 = jnp.where(i >= 1, np.float32(0.0), np.float32(NEG_INF))
    qb = ATT_QBLOCK
    nt = (((1,), (1,)), ((), ()))
    for h in range(N_HEADS):
        sl = slice(h * HEAD_DIM, (h + 1) * HEAD_DIM)
        q = q_ref[:, sl]
        s0 = lax.dot_general(q, k0_ref[:, sl], nt, preferred_element_type=F32) + b_ref[h, :, 0:qb] + pen0
        s1 = lax.dot_general(q, k1_ref[:, sl], nt, preferred_element_type=F32) + b_ref[h, :, qb:2 * qb] + pen1
        s2 = lax.dot_general(q, k2_ref[:, sl], nt, preferred_element_type=F32) + b_ref[h, :, 2 * qb:3 * qb]
        m = jnp.max(jnp.maximum(jnp.maximum(s0, s1), s2), axis=-1, keepdims=True)
        p0 = jnp.exp(s0 - m)
        p1 = jnp.exp(s1 - m)
        p2 = jnp.exp(s2 - m)
        l = jnp.sum(p0 + p1 + p2, axis=-1, keepdims=True)
        acc = jnp.dot(p0.astype(BF16), v0_ref[:, sl], preferred_element_type=F32)
        acc = acc + jnp.dot(p1.astype(BF16), v1_ref[:, sl], preferred_element_type=F32)
        acc = acc + jnp.dot(p2.astype(BF16), v2_ref[:, sl], preferred_element_type=F32)
        o_ref[:, sl] = (acc / l).astype(o_ref.dtype)


def chunked_attention(cols, bias_tab, seq):
    n = cols.shape[0]
    qb = ATT_QBLOCK
    aw = N_HEADS * HEAD_DIM
    bps = seq // qb
    assert ATT_KBLOCKS == 3 and seq % qb == 0

    def kv_spec(col, back):
        return pl.BlockSpec((qb, aw), lambda g: (g - jnp.minimum(g % bps, back), col))

    return pl.pallas_call(
        functools.partial(_attn_kernel, blocks_per_seq=bps),
        grid=(n // qb,),
        in_specs=[pl.BlockSpec((qb, aw), lambda g: (g, 0)),
                  kv_spec(1, 2), kv_spec(1, 1), kv_spec(1, 0),
                  kv_spec(2, 2), kv_spec(2, 1), kv_spec(2, 0),
                  _resident(bias_tab.shape, lambda g: (0, 0, 0))],
        out_specs=pl.BlockSpec((qb, aw), lambda g: (g, 0)),
        out_shape=jax.ShapeDtypeStruct((n, aw), BF16),
        compiler_params=_cparams("arbitrary"),
        name="chunked_attention",
    )(cols, cols, cols, cols, cols, cols, cols, bias_tab)


def _rec_kernel(rx_ref, ry_ref, cw_ref, cb_ref, wg_ref, ba_ref, bx_ref, lam_ref, o_ref,
                xbuf, a_s, u_s, carry, *, rows, width, bw):
    t = pl.program_id(1)
    pad = SUBLANES

    @pl.when(t == 0)
    def _():
        xbuf[0:pad, :] = jnp.zeros((pad, width), F32)
        carry[...] = jnp.zeros((pad, width), F32)

    x = rx_ref[...].astype(F32)
    xbuf[pad:pad + rows, :] = x
    rc = cb_ref[...] + cw_ref[0:1, :] * xbuf[pad - 3:pad - 3 + rows, :]
    rc = rc + cw_ref[1:2, :] * xbuf[pad - 2:pad - 2 + rows, :]
    rc = rc + cw_ref[2:3, :] * xbuf[pad - 1:pad - 1 + rows, :]
    rc = rc + cw_ref[3:4, :] * x
    xbuf[0:pad, :] = xbuf[rows:rows + pad, :]
    u_s[...] = rc

    neg_c_sp = -LRU_C * _softplus(-lam_ref[...])
    for nb in range(width // bw):
        sl = slice(nb * bw, (nb + 1) * bw)
        rcn = u_s[:, sl]
        g = jnp.dot(rcn.astype(BF16), wg_ref[nb], preferred_element_type=F32)
        r = _sigmoid(g[:, :bw] + ba_ref[:, sl])
        ig = _sigmoid(g[:, bw:] + bx_ref[:, sl])
        log_a = neg_c_sp[:, sl] * r
        a_s[:, sl] = jnp.exp(log_a)
        th = jnp.tanh(log_a)
        u_s[:, sl] = jnp.sqrt(-2.0 * th / (1.0 - th)) * (ig * rcn)

    row = lax.broadcasted_iota(jnp.int32, (pad, width), 0)

    def group(gi, c):
        r0 = pl.multiple_of(gi * pad, pad)
        a = a_s[pl.ds(r0, pad), :]
        h = u_s[pl.ds(r0, pad), :]
        for s in (1, 2, 4):
            keep = row >= s
            a_prev = jnp.where(keep, pltpu.roll(a, s, 0), 1.0)
            h_prev = jnp.where(keep, pltpu.roll(h, s, 0), 0.0)
            h = a * h_prev + h
            a = a * a_prev
        h = h + a * c
        u_s[pl.ds(r0, pad), :] = h
        return jnp.broadcast_to(h[pad - 1:pad, :], (pad, width))

    carry[...] = lax.fori_loop(0, rows // pad, group, carry[...])
    o_ref[...] = (_gelu_tanh(ry_ref[...].astype(F32)) * u_s[...]).astype(o_ref.dtype)


def recurrent_branch(cols, conv_w, conv_b, w_gates, b_a, b_x, lam, batch, seq, rx_col, rows=256):
    n = cols.shape[0]
    width = conv_w.shape[1]
    bw = width // LRU_BLOCKS
    spb = seq // rows
    vec = lambda: pl.BlockSpec((1, width), lambda b, t: (0, 0))
    return pl.pallas_call(
        functools.partial(_rec_kernel, rows=rows, width=width, bw=bw),
        grid=(batch, spb),
        in_specs=[pl.BlockSpec((rows, width), lambda b, t: (b * spb + t, rx_col)),
                  pl.BlockSpec((rows, width), lambda b, t: (b * spb + t, rx_col + 1)),
                  pl.BlockSpec((CONV_W, width), lambda b, t: (0, 0)),
                  vec(),
                  pl.BlockSpec((LRU_BLOCKS, bw, 2 * bw), lambda b, t: (0, 0, 0)),
                  vec(), vec(), vec()],
        out_specs=pl.BlockSpec((rows, width), lambda b, t: (b * spb + t, 0)),
        out_shape=jax.ShapeDtypeStruct((n, width), BF16),
        scratch_shapes=[pltpu.VMEM((rows + SUBLANES, width), F32),
                        pltpu.VMEM((rows, width), F32),
                        pltpu.VMEM((rows, width), F32),
                        pltpu.VMEM((SUBLANES, width), F32)],
        compiler_params=_cparams("arbitrary", "arbitrary"),
        name="recurrent_branch",
    )(cols, cols, conv_w, conv_b.reshape(1, width), w_gates,
      b_a.reshape(1, width), b_x.reshape(1, width), lam.reshape(1, width))


def _merge_kernel(att_ref, rec_ref, wa_ref, wr_ref, ga_ref, gb_ref, o_ref):
    ya = jnp.dot(att_ref[...], wa_ref[...], preferred_element_type=F32)
    yr = jnp.dot(rec_ref[...], wr_ref[...], preferred_element_type=F32)
    o = _sigmoid(ga_ref[...].astype(F32)) * ya + _sigmoid(gb_ref[...].astype(F32)) * yr
    o_ref[...] = o.astype(o_ref.dtype)


def gated_merge(att, rec, w_att_o, w_rec_o, cols, ga_col, tm=1024, tn=1024):
    n, aw = att.shape
    rw = rec.shape[1]
    d = w_att_o.shape[1]
    nj = d // tn
    return pl.pallas_call(
        _merge_kernel,
        grid=(n // tm, nj),
        in_specs=[pl.BlockSpec((tm, aw), lambda i, j: (i, 0)),
                  pl.BlockSpec((tm, rw), lambda i, j: (i, 0)),
                  pl.BlockSpec((aw, tn), lambda i, j: (0, j)),
                  pl.BlockSpec((rw, tn), lambda i, j: (0, j)),
                  pl.BlockSpec((tm, tn), lambda i, j: (i, ga_col + j)),
                  pl.BlockSpec((tm, tn), lambda i, j: (i, ga_col + nj + j))],
        out_specs=pl.BlockSpec((tm, tn), lambda i, j: (i, j)),
        out_shape=jax.ShapeDtypeStruct((n, d), BF16),
        compiler_params=_cparams("arbitrary", "arbitrary"),
        name="gated_merge",
    )(att, rec, w_att_o, w_rec_o, cols, cols)


def _proj_res_norm_kernel(x_ref, w_ref, h_ref, g_ref, hout_ref, n_ref, *extra):
    hn = h_ref[...] + jnp.dot(x_ref[...], w_ref[...], preferred_element_type=F32)
    hout_ref[...] = hn
    nrm = _rms(hn) * g_ref[...]
    n_ref[...] = nrm.astype(n_ref.dtype)
    for r in extra:
        r[...] = nrm.astype(r.dtype)


def proj_res_norm(x, w, h, g, norm_dtypes, tm=256):
    n, k = x.shape
    d = w.shape[1]
    row = lambda: pl.BlockSpec((tm, d), lambda i: (i, 0))
    return pl.pallas_call(
        _proj_res_norm_kernel,
        grid=(n // tm,),
        in_specs=[pl.BlockSpec((tm, k), lambda i: (i, 0)),
                  _resident((k, d), lambda i: (0, 0)),
                  row(),
                  pl.BlockSpec((1, d), lambda i: (0, 0))],
        out_specs=[row()] + [row() for _ in norm_dtypes],
        out_shape=[jax.ShapeDtypeStruct((n, d), F32)] + [jax.ShapeDtypeStruct((n, d), dt) for dt in norm_dtypes],
        compiler_params=_cparams("arbitrary"),
        name="proj_res_norm",
    )(x, w, h, g.reshape(1, d))


def _ffn_up_kernel(x_ref, w1_ref, w3_ref, o_ref, w1_bf, w3_bf):
    @pl.when(pl.program_id(1) == 0)
    def _():
        w1_bf[...] = w1_ref[0].astype(BF16)
        w3_bf[...] = w3_ref[0].astype(BF16)

    x = x_ref[...]
    a = jnp.dot(x, w1_bf[...], preferred_element_type=F32)
    b = jnp.dot(x, w3_bf[...], preferred_element_type=F32)
    o_ref[...] = ((a * _sigmoid(a)) * b).astype(o_ref.dtype)


def ffn_up(xn, w1, w3, layer, tm=1024, tn=512):
    n, d = xn.shape
    f = w1.shape[2]
    w_spec = lambda: pl.BlockSpec((1, d, tn), lambda j, i: (layer, 0, j))
    return pl.pallas_call(
        _ffn_up_kernel,
        grid=(f // tn, n // tm),
        in_specs=[pl.BlockSpec((tm, d), lambda j, i: (i, 0)), w_spec(), w_spec()],
        out_specs=pl.BlockSpec((tm, tn), lambda j, i: (i, j)),
        out_shape=jax.ShapeDtypeStruct((n, f), BF16),
        scratch_shapes=[pltpu.VMEM((d, tn), BF16), pltpu.VMEM((d, tn), BF16)],
        compiler_params=_cparams("arbitrary", "arbitrary"),
        name="ffn_up",
    )(xn, w1, w3)


ROUTE_E0, ROUTE_E1, ROUTE_G0, ROUTE_G1, ROUTE_R0, ROUTE_R1 = range(6)


def _router_kernel(x_ref, w_ref, tri_ref, route_ref, cnt_ref, carry):
    @pl.when(pl.program_id(0) == 0)
    def _():
        carry[...] = jnp.zeros_like(carry)

    tm = x_ref.shape[0]
    logits = jnp.dot(x_ref[...], w_ref[...], preferred_element_type=F32)
    lane = lax.broadcasted_iota(jnp.int32, (tm, LANES), 1).astype(F32)
    neg = np.float32(-np.inf)
    lg = jnp.where(lane < N_EXPERTS, logits, neg)
    m0 = jnp.max(lg, axis=-1, keepdims=True)
    e0 = jnp.min(jnp.where(lg == m0, lane, np.float32(LANES)), axis=-1, keepdims=True)
    lg1 = jnp.where(lane == e0, neg, lg)
    m1 = jnp.max(lg1, axis=-1, keepdims=True)
    e1 = jnp.min(jnp.where(lg1 == m1, lane, np.float32(LANES)), axis=-1, keepdims=True)
    ex = jnp.exp(m1 - m0)
    g0 = 1.0 / (1.0 + ex)
    g1 = ex / (1.0 + ex)
    sel0 = lane == e0
    sel1 = lane == e1
    member = jnp.where(sel0 | sel1, np.float32(1.0), np.float32(0.0))
    incl = jnp.dot(tri_ref[...], member.astype(BF16), preferred_element_type=F32)
    before = carry[...] + incl - member
    r0 = jnp.sum(jnp.where(sel0, before, 0.0), axis=-1, keepdims=True)
    r1 = jnp.sum(jnp.where(sel1, before, 0.0), axis=-1, keepdims=True)
    total = carry[...] + incl[tm - 1:tm, :]
    carry[...] = total
    cnt_ref[...] = jnp.broadcast_to(total, cnt_ref.shape)
    out = jnp.zeros((tm, LANES), F32)
    for k, v in ((ROUTE_E0, e0), (ROUTE_E1, e1), (ROUTE_G0, g0), (ROUTE_G1, g1), (ROUTE_R0, r0), (ROUTE_R1, r1)):
        out = jnp.where(lane == k, v, out)
    route_ref[...] = out


def router(hn, w_router, tm=512):
    n, d = hn.shape
    wr = jnp.zeros((d, LANES), BF16).at[:, :N_EXPERTS].set(w_router.astype(BF16))
    tri = jnp.asarray(np.tril(np.ones((tm, tm), np.float32)), BF16)
    return pl.pallas_call(
        _router_kernel,
        grid=(n // tm,),
        in_specs=[pl.BlockSpec((tm, d), lambda i: (i, 0)),
                  pl.BlockSpec((d, LANES), lambda i: (0, 0)),
                  pl.BlockSpec((tm, tm), lambda i: (0, 0))],
        out_specs=[pl.BlockSpec((tm, LANES), lambda i: (i, 0)),
                   pl.BlockSpec((SUBLANES, LANES), lambda i: (0, 0))],
        out_shape=[jax.ShapeDtypeStruct((n, LANES), F32),
                   jax.ShapeDtypeStruct((SUBLANES, LANES), F32)],
        scratch_shapes=[pltpu.VMEM((1, LANES), F32)],
        compiler_params=_cparams("arbitrary"),
        name="router",
    )(hn, wr, tri)


def _dispatch_kernel(dest_ref, zero_ref, x_ref, xs_ref, zbuf, sem, zsem, *, tm):
    step = pl.program_id(0)
    eb = EXPERT_BLOCK

    @pl.when(step == 0)
    def _():
        zbuf[...] = jnp.zeros_like(zbuf)
        for z in range(zero_ref.shape[0]):
            @pl.when(zero_ref[z] >= 0)
            def _():
                cp = pltpu.make_async_copy(zbuf, xs_ref.at[pl.ds(pl.multiple_of(zero_ref[z], eb), eb)], zsem)
                cp.start()
                cp.wait()

    base = step * (tm * TOP_K)

    def row_copy(r, d):
        return pltpu.make_async_copy(x_ref.at[pl.ds(r, 1)], xs_ref.at[pl.ds(d, 1)], sem)

    def issue(r, carry):
        row_copy(r, dest_ref[base + TOP_K * r]).start()
        row_copy(r, dest_ref[base + TOP_K * r + 1]).start()
        return carry

    lax.fori_loop(0, tm, issue, 0, unroll=ROW_DMA_UNROLL)
    for _ in range(TOP_K * tm):
        row_copy(0, 0).wait()


def dispatch(hn32, dest, zero_start, n_rows, tm=256):
    n, d = hn32.shape
    return pl.pallas_call(
        functools.partial(_dispatch_kernel, tm=tm),
        grid_spec=pltpu.PrefetchScalarGridSpec(
            num_scalar_prefetch=2,
            grid=(n // tm,),
            in_specs=[pl.BlockSpec((tm, d), lambda i, dest, tail: (i, 0))],
            out_specs=pl.BlockSpec(memory_space=pl.ANY),
            scratch_shapes=[pltpu.VMEM((EXPERT_BLOCK, d), F32),
                            pltpu.SemaphoreType.DMA,
                            pltpu.SemaphoreType.DMA]),
        out_shape=jax.ShapeDtypeStruct((n_rows, d), F32),
        compiler_params=_cparams("arbitrary"),
        name="moe_dispatch",
    )(dest, zero_start, hn32)


def _grouped_rows_kernel(start_ref, count_ref, spare_ref, x_hbm, *refs, n_weights, tm, tn, tile_fn):
    w_refs = refs[:n_weights]
    o_hbm = refs[n_weights]
    w_bf = refs[n_weights + 1:2 * n_weights + 1]
    xbuf, obuf, zbuf, xsem, osem, zsem = refs[2 * n_weights + 1:]
    j = pl.program_id(0)
    e = pl.program_id(1)
    nb = count_ref[e]
    first = start_ref[e]
    col0 = pl.multiple_of(j * tn, tn)

    def rows_of(blk):
        return pl.ds(pl.multiple_of((first + blk) * tm, tm), tm)

    n_in = xbuf.shape[0]
    n_out = obuf.shape[0]
    tile_bytes = tm * xbuf.shape[2] * xbuf.dtype.itemsize
    n_piece = -(-tile_bytes // MAX_PLAIN_DMA_BYTES)
    piece = tm // n_piece

    def x_copies(blk):
        slot = lax.rem(blk, n_in)
        r0 = (first + blk) * tm
        return [pltpu.make_async_copy(x_hbm.at[pl.ds(pl.multiple_of(r0 + p * piece, piece), piece)],
                                      xbuf.at[slot, pl.ds(p * piece, piece)], xsem.at[slot])
                for p in range(n_piece)]

    def o_copy(blk, slot):
        return pltpu.make_async_copy(obuf.at[slot], o_hbm.at[rows_of(blk), pl.ds(col0, tn)], osem.at[slot])

    def x_start(blk):
        for cp in x_copies(blk):
            cp.start(priority=1)

    def x_wait(blk):
        for cp in x_copies(blk):
            cp.wait()

    @pl.when(nb > 0)
    def _():
        for ahead in range(n_in - 1):
            @pl.when(ahead < nb)
            def _():
                x_start(ahead)
        for wb, w in zip(w_bf, w_refs):
            wb[...] = w[0, 0].astype(BF16)

        def block(i, carry):
            x_wait(i)

            @pl.when(i + (n_in - 1) < nb)
            def _():
                x_start(i + (n_in - 1))

            slot = lax.rem(i, n_out)

            @pl.when(i >= n_out)
            def _():
                o_copy(i - n_out, slot).wait()

            obuf[slot] = tile_fn(xbuf[lax.rem(i, n_in)], *w_bf)
            o_copy(i, slot).start()
            return carry

        lax.fori_loop(0, nb, block, 0)

        for back in range(n_out, 0, -1):
            @pl.when(nb >= back)
            def _():
                o_copy(nb - back, lax.rem(nb - back, n_out)).wait()

    @pl.when(e == pl.num_programs(1) - 1)
    def _():
        zbuf[...] = jnp.zeros_like(zbuf)

        def z_copy(z):
            rows = pl.ds(pl.multiple_of(spare_ref[z], tm), tm)
            return pltpu.make_async_copy(zbuf, o_hbm.at[rows, pl.ds(col0, tn)], zsem)

        for z in range(spare_ref.shape[0]):
            @pl.when(spare_ref[z] >= 0)
            def _():
                z_copy(z).start()
        for z in range(spare_ref.shape[0]):
            @pl.when(spare_ref[z] >= 0)
            def _():
                z_copy(z).wait()


def _grouped_rows(x, weights, layer, groups, out_dtype, tn, tile_fn, name):
    start_blk, count_blk, spare_start = groups
    rows, k = x.shape
    n_exp = weights[0].shape[1]
    width = weights[0].shape[3]
    tm = EXPERT_BLOCK
    w_spec = lambda: pl.BlockSpec((1, 1, k, tn), lambda j, e, *_: (layer, e, 0, j))
    return pl.pallas_call(
        functools.partial(_grouped_rows_kernel, n_weights=len(weights), tm=tm, tn=tn, tile_fn=tile_fn),
        grid_spec=pltpu.PrefetchScalarGridSpec(
            num_scalar_prefetch=3,
            grid=(width // tn, n_exp),
            in_specs=[pl.BlockSpec(memory_space=pl.ANY)] + [w_spec() for _ in weights],
            out_specs=pl.BlockSpec(memory_space=pl.ANY),
            scratch_shapes=[pltpu.VMEM((k, tn), BF16) for _ in weights] + [
                pltpu.VMEM((GROUPED_IN_SLOTS, tm, k), x.dtype),
                pltpu.VMEM((GROUPED_OUT_SLOTS, tm, tn), out_dtype),
                pltpu.VMEM((tm, tn), out_dtype),
                pltpu.SemaphoreType.DMA((GROUPED_IN_SLOTS,)),
                pltpu.SemaphoreType.DMA((GROUPED_OUT_SLOTS,)),
                pltpu.SemaphoreType.DMA]),
        out_shape=jax.ShapeDtypeStruct((rows, width), out_dtype),
        compiler_params=_cparams("arbitrary", "arbitrary"),
        name=name,
    )(start_blk, count_blk, spare_start, x, *weights)


def _swiglu_tile(x, w1_bf, w3_bf):
    xb = x.astype(BF16)
    a = jnp.dot(xb, w1_bf[...], preferred_element_type=F32)
    b = jnp.dot(xb, w3_bf[...], preferred_element_type=F32)
    return ((a * _sigmoid(a)) * b).astype(BF16)


def _matmul_tile(x, w_bf):
    return jnp.dot(x, w_bf[...], preferred_element_type=F32)


def moe_up(xs, w1, w3, layer, groups, tn=1024):
    return _grouped_rows(xs, (w1, w3), layer, groups, BF16, tn, _swiglu_tile, "moe_up")


def moe_down(act, w2, layer, groups, tn=512):
    return _grouped_rows(act, (w2,), layer, groups, F32, tn, _matmul_tile, "moe_down")


def _combine_kernel(dest_ref, ys_ref, route_ref, h_ref, g_ref, hout_ref, n_ref, buf0, buf1, sems, *, tm):
    base = pl.program_id(0) * (tm * TOP_K)

    def row_copy(buf, slot, r, d):
        return pltpu.make_async_copy(ys_ref.at[pl.ds(d, 1)], buf.at[pl.ds(r, 1)], sems.at[slot])

    def issue(r, carry):
        row_copy(buf0, 0, r, dest_ref[base + TOP_K * r]).start()
        row_copy(buf1, 1, r, dest_ref[base + TOP_K * r + 1]).start()
        return carry

    lax.fori_loop(0, tm, issue, 0, unroll=ROW_DMA_UNROLL)
    for _ in range(tm):
        row_copy(buf0, 0, 0, 0).wait()
    for _ in range(tm):
        row_copy(buf1, 1, 0, 0).wait()

    route = route_ref[...]
    g0 = route[:, ROUTE_G0:ROUTE_G0 + 1]
    g1 = route[:, ROUTE_G1:ROUTE_G1 + 1]
    hn = h_ref[...] + (g0 * buf0[...] + g1 * buf1[...])
    hout_ref[...] = hn
    n_ref[...] = (_rms(hn) * g_ref[...]).astype(n_ref.dtype)


def combine(ys, dest, route, h, g, norm_dtype, tm=256):
    n, d = h.shape
    row = lambda: pl.BlockSpec((tm, d), lambda i, dest: (i, 0))
    return pl.pallas_call(
        functools.partial(_combine_kernel, tm=tm),
        grid_spec=pltpu.PrefetchScalarGridSpec(
            num_scalar_prefetch=1,
            grid=(n // tm,),
            in_specs=[pl.BlockSpec(memory_space=pl.ANY),
                      pl.BlockSpec((tm, LANES), lambda i, dest: (i, 0)),
                      row(),
                      pl.BlockSpec((1, d), lambda i, dest: (0, 0))],
            out_specs=[row(), row()],
            scratch_shapes=[pltpu.VMEM((tm, d), F32),
                            pltpu.VMEM((tm, d), F32),
                            pltpu.SemaphoreType.DMA((TOP_K,))]),
        out_shape=[jax.ShapeDtypeStruct((n, d), F32), jax.ShapeDtypeStruct((n, d), norm_dtype)],
        compiler_params=_cparams("arbitrary"),
        name="moe_combine",
    )(dest, ys, route, h, g.reshape(1, d))


def moe_layer(hn_bf, hn32, h, w_router, w1, w3, w2, layer, g_next, norm_dtype):
    n = h.shape[0]
    eb = EXPERT_BLOCK
    route, cnt = router(hn_bf, w_router)
    counts = cnt[0, :N_EXPERTS].astype(jnp.int32)
    padded = (counts + eb - 1) // eb * eb
    pend = jnp.cumsum(padded)
    pstart = pend - padded
    experts = route[:, ROUTE_E0:ROUTE_E1 + 1].astype(jnp.int32)
    ranks = route[:, ROUTE_R0:ROUTE_R1 + 1].astype(jnp.int32)
    dest = (pstart[experts] + ranks).reshape(-1)
    n_blocks = -(-(n * TOP_K) // eb) + N_EXPERTS
    tail_start = jnp.where(padded > 0, pend - eb, -1)
    spare = pend[-1] + jnp.arange(N_EXPERTS) * eb
    spare_start = jnp.where(spare < n_blocks * eb, spare, -1).astype(jnp.int32)
    zero_start = jnp.concatenate([tail_start.astype(jnp.int32), spare_start])
    groups = ((pstart // eb).astype(jnp.int32), (padded // eb).astype(jnp.int32), spare_start)
    xs = dispatch(hn32, dest, zero_start, n_blocks * eb)
    act = moe_up(xs, w1, w3, layer, groups)
    ys = moe_down(act, w2, layer, groups)
    return combine(ys, dest, route, h, g_next, norm_dtype)


def kernel(x, norm_mix_g, norm_ffn_g, norm_final_g, w_in, rel_bias, conv_w, conv_b, lru_wa, lru_ba, lru_wx, lru_bx, lru_lambda, w_att_o, w_rec_o, w_out, ffn_w1, ffn_w3, ffn_w2, router_w, moe_w1, moe_w3, moe_w2):
    b, s, d = x.shape
    n = b * s
    depth = w_in.shape[0]
    aw = N_HEADS * HEAD_DIM
    lw = conv_w.shape[2]
    assert lw == aw and d % aw == 0, "column-block indexing assumes equal branch widths"
    h = x.reshape(n, d)
    xn = rmsnorm(h, norm_mix_g[0], BF16)
    out = None
    for l in range(depth):
        cols = in_projection(xn, w_in, l, aw, HEAD_DIM ** -0.5)
        att = chunked_attention(cols, _rel_bias_table(rel_bias[l]), s)
        w_gates = jnp.concatenate([lru_wa[l], lru_wx[l]], axis=-1).astype(BF16)
        rec = recurrent_branch(cols, conv_w[l], conv_b[l], w_gates, lru_ba[l], lru_bx[l], lru_lambda[l],
                               b, s, rx_col=3)
        merged = gated_merge(att, rec, w_att_o[l].astype(BF16), w_rec_o[l].astype(BF16), cols, ga_col=5)
        last = l == depth - 1
        g_next = norm_final_g if last else norm_mix_g[l + 1]
        next_dtype = F32 if last else BF16
        i = l // 2
        if l % 2 == 0:
            h, hn = proj_res_norm(merged, w_out[l].astype(BF16), h, norm_ffn_g[l], [BF16])
            act = ffn_up(hn, ffn_w1, ffn_w3, i)
            h, xn = proj_res_norm(act, ffn_w2[i].astype(BF16), h, g_next, [next_dtype])
        else:
            h, hn, hn32 = proj_res_norm(merged, w_out[l].astype(BF16), h, norm_ffn_g[l], [BF16, F32])
            h, xn = moe_layer(hn, hn32, h, router_w[i], moe_w1, moe_w3, moe_w2, i, g_next, next_dtype)
        out = xn
    return out.reshape(b, s, d)
```

```python
import functools

import numpy as np
import jax
import jax.numpy as jnp
from jax import lax
from jax.experimental import pallas as pl
from jax.experimental.pallas import tpu as pltpu

CHUNK = 64
LEFT_CHUNKS = 8
N_HEADS = 8
HEAD_DIM = 128
MAX_REL_DIST = 256
LRU_BLOCKS = 8
CONV_W = 4
LRU_C = 8.0
N_EXPERTS = 8
TOP_K = 2
EXPERT_BLOCK = 256
RMS_EPS = 1e-6
NEG_INF = -1e30
LOG2_E = float(np.log2(np.e))

LANES = 128
SUBLANES = 8
VMEM_LIMIT_BYTES = 56 * 1024 * 1024

ATT_QBLOCK = 4 * CHUNK
ATT_KBLOCKS = (LEFT_CHUNKS * CHUNK) // ATT_QBLOCK + 1

ROW_DMA_UNROLL = 8

GROUPED_UNIT_BLOCKS = 2
GROUPED_IN_SLOTS = 2
GROUPED_OUT_SLOTS = 2
MAX_PLAIN_DMA_BYTES = 2 * 1024 * 1024

F32 = jnp.float32
BF16 = jnp.bfloat16


def _cparams(*sem):
    return pltpu.CompilerParams(dimension_semantics=sem, vmem_limit_bytes=VMEM_LIMIT_BYTES)


def _resident(shape, index_map):
    return pl.BlockSpec(shape, index_map, pipeline_mode=pl.Buffered(1))


def _rms(x):
    return x * lax.rsqrt(jnp.mean(x * x, axis=-1, keepdims=True) + RMS_EPS)


def _sigmoid(x):
    return 1.0 / (1.0 + jnp.exp(-x))


def _gelu_tanh(x):
    c = np.float32(np.sqrt(2.0 / np.pi))
    return 0.5 * x * (1.0 + jnp.tanh(c * (x + 0.044715 * (x * x * x))))


def _softplus(x):
    return jnp.maximum(x, 0.0) + jnp.log1p(jnp.exp(-jnp.abs(x)))


def _rmsnorm_kernel(x_ref, g_ref, o_ref):
    o_ref[...] = (_rms(x_ref[...]) * g_ref[...]).astype(o_ref.dtype)


def rmsnorm(x, g, out_dtype, tm=512):
    n, d = x.shape
    return pl.pallas_call(
        _rmsnorm_kernel,
        grid=(n // tm,),
        in_specs=[pl.BlockSpec((tm, d), lambda i: (i, 0)),
                  pl.BlockSpec((1, d), lambda i: (0, 0))],
        out_specs=pl.BlockSpec((tm, d), lambda i: (i, 0)),
        out_shape=jax.ShapeDtypeStruct((n, d), out_dtype),
        compiler_params=_cparams("arbitrary"),
        name="rmsnorm",
    )(x, g.reshape(1, d))


def _inproj_kernel(x_ref, w_ref, o_ref, w_bf, *, q_scale):
    @pl.when(pl.program_id(1) == 0)
    def _():
        w_bf[...] = w_ref[0].astype(BF16)

    acc = jnp.dot(x_ref[...], w_bf[...], preferred_element_type=F32)
    s = jnp.where(pl.program_id(0) == 0, np.float32(q_scale), np.float32(1.0))
    o_ref[...] = (acc * s).astype(o_ref.dtype)


def in_projection(xn, w, layer, q_width, q_scale, tm=1024):
    n, d = xn.shape
    width = w.shape[2]
    tn = q_width
    return pl.pallas_call(
        functools.partial(_inproj_kernel, q_scale=q_scale),
        grid=(width // tn, n // tm),
        in_specs=[pl.BlockSpec((tm, d), lambda j, i: (i, 0)),
                  pl.BlockSpec((1, d, tn), lambda j, i: (layer, 0, j))],
        out_specs=pl.BlockSpec((tm, tn), lambda j, i: (i, j)),
        out_shape=jax.ShapeDtypeStruct((n, width), BF16),
        scratch_shapes=[pltpu.VMEM((d, tn), BF16)],
        compiler_params=_cparams("arbitrary", "arbitrary"),
        name="in_projection",
    )(xn, w)


def _rel_bias_table(rel_bias):
    qb = ATT_QBLOCK
    kw = ATT_KBLOCKS * ATT_QBLOCK
    n_heads, n_rel = rel_bias.shape
    period = qb + kw
    dist = LEFT_CHUNKS * CHUNK + (qb - 1) - np.arange(period)
    idx = np.clip(dist, -(CHUNK - 1), MAX_REL_DIST) + CHUNK - 1
    n_far = int(np.sum(idx == n_rel - 1)) - 1
    n_near = int(np.sum(idx == 0)) - 1
    assert np.array_equal(idx, np.concatenate([np.full(n_far, n_rel - 1), np.arange(n_rel)[::-1], np.full(n_near, 0)]))
    rb = rel_bias.astype(F32)
    diag = jnp.concatenate([jnp.broadcast_to(rb[:, -1:], (n_heads, n_far)), rb[:, ::-1],
                            jnp.broadcast_to(rb[:, :1], (n_heads, n_near))], axis=1)
    rep = jnp.broadcast_to(diag[:, None, :], (n_heads, qb, period)).reshape(n_heads, qb * period)
    skew = rep[:, :qb * (period - 1)].reshape(n_heads, qb, period - 1)
    tab = skew[:, :, qb - 1:qb - 1 + kw]
    qc = np.arange(qb)[:, None] // CHUNK
    kc = np.arange(kw)[None, :] // CHUNK
    valid = (kc >= qc) & (kc <= qc + LEFT_CHUNKS)
    return jnp.where(valid[None], tab * LOG2_E, NEG_INF)


def _attn_heads(q_ref, k_refs, v_refs, b_ref, o_ref, penalties):
    qb = ATT_QBLOCK
    nt = (((1,), (1,)), ((), ()))
    for h in range(N_HEADS):
        sl = slice(h * HEAD_DIM, (h + 1) * HEAD_DIM)
        q = q_ref[:, sl]
        s = []
        for c, k_ref in enumerate(k_refs):
            sc = lax.dot_general(q, k_ref[:, sl], nt, preferred_element_type=F32) + b_ref[h, :, c * qb:(c + 1) * qb]
            if penalties is not None and c < len(penalties):
                sc = sc + penalties[c]
            s.append(sc)
        m = jnp.max(functools.reduce(jnp.maximum, s), axis=-1, keepdims=True)
        p = [jnp.exp2(sc - m) for sc in s]
        l = jnp.sum(functools.reduce(jnp.add, p), axis=-1, keepdims=True)
        acc = None
        for pc, v_ref in zip(p, v_refs):
            pv = jnp.dot(pc.astype(BF16), v_ref[:, sl], preferred_element_type=F32)
            acc = pv if acc is None else acc + pv
        o_ref[:, sl] = (acc / l).astype(o_ref.dtype)


def _attn_kernel(q_ref, k0_ref, k1_ref, k2_ref, v0_ref, v1_ref, v2_ref, b_ref, o_ref, *, blocks_per_seq):
    i = pl.program_id(0) % blocks_per_seq
    k_refs = (k0_ref, k1_ref, k2_ref)
    v_refs = (v0_ref, v1_ref, v2_ref)
    first = i < ATT_KBLOCKS - 1

    @pl.when(first)
    def _():
        pen = [jnp.where(i >= ATT_KBLOCKS - 1 - c, np.float32(0.0), np.float32(NEG_INF))
               for c in range(ATT_KBLOCKS - 1)]
        _attn_heads(q_ref, k_refs, v_refs, b_ref, o_ref, pen)

    @pl.when(jnp.logical_not(first))
    def _():
        _attn_heads(q_ref, k_refs, v_refs, b_ref, o_ref, None)


def chunked_attention(cols, bias_tab, seq):
    n = cols.shape[0]
    qb = ATT_QBLOCK
    aw = N_HEADS * HEAD_DIM
    bps = seq // qb
    assert ATT_KBLOCKS == 3 and seq % qb == 0

    def kv_spec(col, back):
        return pl.BlockSpec((qb, aw), lambda g: (g - jnp.minimum(g % bps, back), col))

    return pl.pallas_call(
        functools.partial(_attn_kernel, blocks_per_seq=bps),
        grid=(n // qb,),
        in_specs=[pl.BlockSpec((qb, aw), lambda g: (g, 0)),
                  kv_spec(1, 2), kv_spec(1, 1), kv_spec(1, 0),
                  kv_spec(2, 2), kv_spec(2, 1), kv_spec(2, 0),
                  _resident(bias_tab.shape, lambda g: (0, 0, 0))],
        out_specs=pl.BlockSpec((qb, aw), lambda g: (g, 0)),
        out_shape=jax.ShapeDtypeStruct((n, aw), BF16),
        compiler_params=_cparams("arbitrary"),
        name="chunked_attention",
    )(cols, cols, cols, cols, cols, cols, cols, bias_tab)


def _rec_kernel(rx_ref, ry_ref, cw_ref, cb_ref, wg_ref, ba_ref, bx_ref, lam_ref, o_ref,
                xbuf, a_s, u_s, carry, *, rows, width, bw):
    t = pl.program_id(1)
    pad = SUBLANES

    @pl.when(t == 0)
    def _():
        xbuf[0:pad, :] = jnp.zeros((pad, width), F32)
        carry[...] = jnp.zeros((pad, width), F32)

    x = rx_ref[...].astype(F32)
    xbuf[pad:pad + rows, :] = x
    rc = cb_ref[...] + cw_ref[0:1, :] * xbuf[pad - 3:pad - 3 + rows, :]
    rc = rc + cw_ref[1:2, :] * xbuf[pad - 2:pad - 2 + rows, :]
    rc = rc + cw_ref[2:3, :] * xbuf[pad - 1:pad - 1 + rows, :]
    rc = rc + cw_ref[3:4, :] * x
    xbuf[0:pad, :] = xbuf[rows:rows + pad, :]
    u_s[...] = rc

    neg_c_sp = -LRU_C * _softplus(-lam_ref[...])
    for nb in range(width // bw):
        sl = slice(nb * bw, (nb + 1) * bw)
        rcn = u_s[:, sl]
        g = jnp.dot(rcn.astype(BF16), wg_ref[nb], preferred_element_type=F32)
        r = _sigmoid(g[:, :bw] + ba_ref[:, sl])
        ig = _sigmoid(g[:, bw:] + bx_ref[:, sl])
        log_a = neg_c_sp[:, sl] * r
        a_s[:, sl] = jnp.exp(log_a)
        th = jnp.tanh(log_a)
        one_m_a2 = -2.0 * th / (1.0 - th)
        root = one_m_a2 * lax.rsqrt(jnp.maximum(one_m_a2, np.float32(1e-30)))
        u_s[:, sl] = root * (ig * rcn)

    row = lax.broadcasted_iota(jnp.int32, (pad, width), 0)

    def group(gi, c):
        r0 = pl.multiple_of(gi * pad, pad)
        a = a_s[pl.ds(r0, pad), :]
        h = u_s[pl.ds(r0, pad), :]
        for s in (1, 2, 4):
            keep = row >= s
            a_prev = jnp.where(keep, pltpu.roll(a, s, 0), 1.0)
            h_prev = jnp.where(keep, pltpu.roll(h, s, 0), 0.0)
            h = a * h_prev + h
            a = a * a_prev
        h = h + a * c
        u_s[pl.ds(r0, pad), :] = h
        return jnp.broadcast_to(h[pad - 1:pad, :], (pad, width))

    carry[...] = lax.fori_loop(0, rows // pad, group, carry[...])
    o_ref[...] = (_gelu_tanh(ry_ref[...].astype(F32)) * u_s[...]).astype(o_ref.dtype)


def recurrent_branch(cols, conv_w, conv_b, w_gates, b_a, b_x, lam, batch, seq, rx_col, rows=256):
    n = cols.shape[0]
    width = conv_w.shape[1]
    bw = width // LRU_BLOCKS
    spb = seq // rows
    vec = lambda: pl.BlockSpec((1, width), lambda b, t: (0, 0))
    return pl.pallas_call(
        functools.partial(_rec_kernel, rows=rows, width=width, bw=bw),
        grid=(batch, spb),
        in_specs=[pl.BlockSpec((rows, width), lambda b, t: (b * spb + t, rx_col)),
                  pl.BlockSpec((rows, width), lambda b, t: (b * spb + t, rx_col + 1)),
                  pl.BlockSpec((CONV_W, width), lambda b, t: (0, 0)),
                  vec(),
                  pl.BlockSpec((LRU_BLOCKS, bw, 2 * bw), lambda b, t: (0, 0, 0)),
                  vec(), vec(), vec()],
        out_specs=pl.BlockSpec((rows, width), lambda b, t: (b * spb + t, 0)),
        out_shape=jax.ShapeDtypeStruct((n, width), BF16),
        scratch_shapes=[pltpu.VMEM((rows + SUBLANES, width), F32),
                        pltpu.VMEM((rows, width), F32),
                        pltpu.VMEM((rows, width), F32),
                        pltpu.VMEM((SUBLANES, width), F32)],
        compiler_params=_cparams("arbitrary", "arbitrary"),
        name="recurrent_branch",
    )(cols, cols, conv_w, conv_b.reshape(1, width), w_gates,
      b_a.reshape(1, width), b_x.reshape(1, width), lam.reshape(1, width))


def _merge_kernel(att_ref, rec_ref, wa_ref, wr_ref, ga_ref, gb_ref, o_ref):
    ya = jnp.dot(att_ref[...], wa_ref[...], preferred_element_type=F32)
    yr = jnp.dot(rec_ref[...], wr_ref[...], preferred_element_type=F32)
    o = _sigmoid(ga_ref[...].astype(F32)) * ya + _sigmoid(gb_ref[...].astype(F32)) * yr
    o_ref[...] = o.astype(o_ref.dtype)


def gated_merge(att, rec, w_att_o, w_rec_o, cols, ga_col, tm=1024, tn=1024):
    n, aw = att.shape
    rw = rec.shape[1]
    d = w_att_o.shape[1]
    nj = d // tn
    return pl.pallas_call(
        _merge_kernel,
        grid=(n // tm, nj),
        in_specs=[pl.BlockSpec((tm, aw), lambda i, j: (i, 0)),
                  pl.BlockSpec((tm, rw), lambda i, j: (i, 0)),
                  pl.BlockSpec((aw, tn), lambda i, j: (0, j)),
                  pl.BlockSpec((rw, tn), lambda i, j: (0, j)),
                  pl.BlockSpec((tm, tn), lambda i, j: (i, ga_col + j)),
                  pl.BlockSpec((tm, tn), lambda i, j: (i, ga_col + nj + j))],
        out_specs=pl.BlockSpec((tm, tn), lambda i, j: (i, j)),
        out_shape=jax.ShapeDtypeStruct((n, d), BF16),
        compiler_params=_cparams("arbitrary", "arbitrary"),
        name="gated_merge",
    )(att, rec, w_att_o, w_rec_o, cols, cols)


def _proj_res_norm_kernel(x_ref, w_ref, h_ref, g_ref, hout_ref, n_ref, *extra):
    hn = h_ref[...] + jnp.dot(x_ref[...], w_ref[...], preferred_element_type=F32)
    hout_ref[...] = hn
    nrm = _rms(hn) * g_ref[...]
    n_ref[...] = nrm.astype(n_ref.dtype)
    for r in extra:
        r[...] = nrm.astype(r.dtype)


def proj_res_norm(x, w, h, g, norm_dtypes, tm=256):
    n, k = x.shape
    d = w.shape[1]
    row = lambda: pl.BlockSpec((tm, d), lambda i: (i, 0))
    return pl.pallas_call(
        _proj_res_norm_kernel,
        grid=(n // tm,),
        in_specs=[pl.BlockSpec((tm, k), lambda i: (i, 0)),
                  _resident((k, d), lambda i: (0, 0)),
                  row(),
                  pl.BlockSpec((1, d), lambda i: (0, 0))],
        out_specs=[row()] + [row() for _ in norm_dtypes],
        out_shape=[jax.ShapeDtypeStruct((n, d), F32)] + [jax.ShapeDtypeStruct((n, d), dt) for dt in norm_dtypes],
        compiler_params=_cparams("arbitrary"),
        name="proj_res_norm",
    )(x, w, h, g.reshape(1, d))


def _ffn_up_kernel(x_ref, w1_ref, w3_ref, o_ref, w1_bf, w3_bf):
    @pl.when(pl.program_id(1) == 0)
    def _():
        w1_bf[...] = w1_ref[0].astype(BF16)
        w3_bf[...] = w3_ref[0].astype(BF16)

    x = x_ref[...]
    a = jnp.dot(x, w1_bf[...], preferred_element_type=F32)
    b = jnp.dot(x, w3_bf[...], preferred_element_type=F32)
    o_ref[...] = ((a * _sigmoid(a)) * b).astype(o_ref.dtype)


def ffn_up(xn, w1, w3, layer, tm=1024, tn=512):
    n, d = xn.shape
    f = w1.shape[2]
    w_spec = lambda: pl.BlockSpec((1, d, tn), lambda j, i: (layer, 0, j))
    return pl.pallas_call(
        _ffn_up_kernel,
        grid=(f // tn, n // tm),
        in_specs=[pl.BlockSpec((tm, d), lambda j, i: (i, 0)), w_spec(), w_spec()],
        out_specs=pl.BlockSpec((tm, tn), lambda j, i: (i, j)),
        out_shape=jax.ShapeDtypeStruct((n, f), BF16),
        scratch_shapes=[pltpu.VMEM((d, tn), BF16), pltpu.VMEM((d, tn), BF16)],
        compiler_params=_cparams("arbitrary", "arbitrary"),
        name="ffn_up",
    )(xn, w1, w3)


ROUTE_E0, ROUTE_E1, ROUTE_G0, ROUTE_G1, ROUTE_R0, ROUTE_R1 = range(6)


def _router_kernel(x_ref, w_ref, tri_ref, route_ref, cnt_ref, carry):
    @pl.when(pl.program_id(0) == 0)
    def _():
        carry[...] = jnp.zeros_like(carry)

    tm = x_ref.shape[0]
    logits = jnp.dot(x_ref[...], w_ref[...], preferred_element_type=F32)
    lane = lax.broadcasted_iota(jnp.int32, (tm, LANES), 1).astype(F32)
    neg = np.float32(-np.inf)
    lg = jnp.where(lane < N_EXPERTS, logits, neg)
    m0 = jnp.max(lg, axis=-1, keepdims=True)
    e0 = jnp.min(jnp.where(lg == m0, lane, np.float32(LANES)), axis=-1, keepdims=True)
    lg1 = jnp.where(lane == e0, neg, lg)
    m1 = jnp.max(lg1, axis=-1, keepdims=True)
    e1 = jnp.min(jnp.where(lg1 == m1, lane, np.float32(LANES)), axis=-1, keepdims=True)
    ex = jnp.exp(m1 - m0)
    g0 = 1.0 / (1.0 + ex)
    g1 = ex / (1.0 + ex)
    sel0 = lane == e0
    sel1 = lane == e1
    member = jnp.where(sel0 | sel1, np.float32(1.0), np.float32(0.0))
    incl = jnp.dot(tri_ref[...], member.astype(BF16), preferred_element_type=F32)
    before = carry[...] + incl - member
    r0 = jnp.sum(jnp.where(sel0, before, 0.0), axis=-1, keepdims=True)
    r1 = jnp.sum(jnp.where(sel1, before, 0.0), axis=-1, keepdims=True)
    total = carry[...] + incl[tm - 1:tm, :]
    carry[...] = total
    cnt_ref[...] = jnp.broadcast_to(total, cnt_ref.shape)
    out = jnp.zeros((tm, LANES), F32)
    for k, v in ((ROUTE_E0, e0), (ROUTE_E1, e1), (ROUTE_G0, g0), (ROUTE_G1, g1), (ROUTE_R0, r0), (ROUTE_R1, r1)):
        out = jnp.where(lane == k, v, out)
    route_ref[...] = out


def router(hn, w_router, tm=512):
    n, d = hn.shape
    wr = jnp.zeros((d, LANES), BF16).at[:, :N_EXPERTS].set(w_router.astype(BF16))
    tri = jnp.asarray(np.tril(np.ones((tm, tm), np.float32)), BF16)
    return pl.pallas_call(
        _router_kernel,
        grid=(n // tm,),
        in_specs=[pl.BlockSpec((tm, d), lambda i: (i, 0)),
                  pl.BlockSpec((d, LANES), lambda i: (0, 0)),
                  pl.BlockSpec((tm, tm), lambda i: (0, 0))],
        out_specs=[pl.BlockSpec((tm, LANES), lambda i: (i, 0)),
                   pl.BlockSpec((SUBLANES, LANES), lambda i: (0, 0))],
        out_shape=[jax.ShapeDtypeStruct((n, LANES), F32),
                   jax.ShapeDtypeStruct((SUBLANES, LANES), F32)],
        scratch_shapes=[pltpu.VMEM((1, LANES), F32)],
        compiler_params=_cparams("arbitrary"),
        name="router",
    )(hn, wr, tri)


def _dispatch_kernel(dest_ref, zero_ref, x_ref, xs_ref, zbuf, sem, zsem, *, tm):
    step = pl.program_id(0)
    eb = EXPERT_BLOCK

    @pl.when(step == 0)
    def _():
        zbuf[...] = jnp.zeros_like(zbuf)
        for z in range(zero_ref.shape[0]):
            @pl.when(zero_ref[z] >= 0)
            def _():
                cp = pltpu.make_async_copy(zbuf, xs_ref.at[pl.ds(pl.multiple_of(zero_ref[z], eb), eb)], zsem)
                cp.start()
                cp.wait()

    base = step * (tm * TOP_K)

    def row_copy(r, d):
        return pltpu.make_async_copy(x_ref.at[pl.ds(r, 1)], xs_ref.at[pl.ds(d, 1)], sem)

    def issue(r, carry):
        row_copy(r, dest_ref[base + TOP_K * r]).start()
        row_copy(r, dest_ref[base + TOP_K * r + 1]).start()
        return carry

    lax.fori_loop(0, tm, issue, 0, unroll=ROW_DMA_UNROLL)
    for _ in range(TOP_K * tm):
        row_copy(0, 0).wait()


def dispatch(hn32, dest, zero_start, n_rows, tm=256):
    n, d = hn32.shape
    return pl.pallas_call(
        functools.partial(_dispatch_kernel, tm=tm),
        grid_spec=pltpu.PrefetchScalarGridSpec(
            num_scalar_prefetch=2,
            grid=(n // tm,),
            in_specs=[pl.BlockSpec((tm, d), lambda i, dest, tail: (i, 0))],
            out_specs=pl.BlockSpec(memory_space=pl.ANY),
            scratch_shapes=[pltpu.VMEM((EXPERT_BLOCK, d), F32),
                            pltpu.SemaphoreType.DMA,
                            pltpu.SemaphoreType.DMA]),
        out_shape=jax.ShapeDtypeStruct((n_rows, d), F32),
        compiler_params=_cparams("arbitrary"),
        name="moe_dispatch",
    )(dest, zero_start, hn32)


def _grouped_rows_kernel(start_ref, count_ref, spare_ref, x_hbm, *refs, n_weights, tm, tn, tile_fn):
    w_refs = refs[:n_weights]
    o_hbm = refs[n_weights]
    w_bf = refs[n_weights + 1:2 * n_weights + 1]
    xbuf, obuf, zbuf, xsem, osem, zsem = refs[2 * n_weights + 1:]
    j = pl.program_id(0)
    e = pl.program_id(1)
    nb = count_ref[e]
    first = start_ref[e]
    col0 = pl.multiple_of(j * tn, tn)

    n_in = xbuf.shape[0]
    n_out = obuf.shape[0]
    ub = xbuf.shape[1] // tm
    n_units = lax.div(nb + (ub - 1), ub)
    n_piece = -(-(tm * xbuf.shape[2] * xbuf.dtype.itemsize) // MAX_PLAIN_DMA_BYTES)
    piece = tm // n_piece

    def x_copies(u, b):
        slot = lax.rem(u, n_in)
        r0 = (first + u * ub + b) * tm
        return [pltpu.make_async_copy(x_hbm.at[pl.ds(pl.multiple_of(r0 + p * piece, piece), piece)],
                                      xbuf.at[slot, pl.ds(b * tm + p * piece, piece)], xsem.at[slot])
                for p in range(n_piece)]

    def o_copy(u, b):
        slot = lax.rem(u, n_out)
        rows = pl.ds(pl.multiple_of((first + u * ub + b) * tm, tm), tm)
        return pltpu.make_async_copy(obuf.at[slot, pl.ds(b * tm, tm)], o_hbm.at[rows, pl.ds(col0, tn)],
                                     osem.at[slot])

    def for_blocks(u, fn):
        fn(0)
        for b in range(1, ub):
            @pl.when(u * ub + b < nb)
            def _():
                fn(b)

    def x_start(u):
        for_blocks(u, lambda b: [cp.start(priority=1) for cp in x_copies(u, b)])

    def x_wait(u):
        for_blocks(u, lambda b: [cp.wait() for cp in x_copies(u, b)])

    @pl.when(nb > 0)
    def _():
        for ahead in range(n_in - 1):
            @pl.when(ahead < n_units)
            def _():
                x_start(ahead)
        for wb, w in zip(w_bf, w_refs):
            wb[...] = w[0, 0].astype(BF16)

        def unit(u, carry):
            x_wait(u)

            @pl.when(u + (n_in - 1) < n_units)
            def _():
                x_start(u + (n_in - 1))

            @pl.when(u >= n_out)
            def _():
                for_blocks(u - n_out, lambda b: o_copy(u - n_out, b).wait())

            slot_in = lax.rem(u, n_in)
            slot_out = lax.rem(u, n_out)
            have = nb - u * ub

            @pl.when(have >= ub)
            def _():
                obuf[slot_out] = tile_fn(xbuf[slot_in], *w_bf)

            for short in range(1, ub):
                @pl.when(have == short)
                def _():
                    obuf[slot_out, :short * tm] = tile_fn(xbuf[slot_in, :short * tm], *w_bf)

            for_blocks(u, lambda b: o_copy(u, b).start())
            return carry

        lax.fori_loop(0, n_units, unit, 0)

        for back in range(n_out, 0, -1):
            @pl.when(n_units >= back)
            def _():
                for_blocks(n_units - back, lambda b: o_copy(n_units - back, b).wait())

    @pl.when(e == pl.num_programs(1) - 1)
    def _():
        zbuf[...] = jnp.zeros_like(zbuf)

        def z_copy(z):
            rows = pl.ds(pl.multiple_of(spare_ref[z], tm), tm)
            return pltpu.make_async_copy(zbuf, o_hbm.at[rows, pl.ds(col0, tn)], zsem)

        for z in range(spare_ref.shape[0]):
            @pl.when(spare_ref[z] >= 0)
            def _():
                z_copy(z).start()
        for z in range(spare_ref.shape[0]):
            @pl.when(spare_ref[z] >= 0)
            def _():
                z_copy(z).wait()


def _grouped_rows(x, weights, layer, groups, out_dtype, tn, tile_fn, name):
    start_blk, count_blk, spare_start = groups
    rows, k = x.shape
    n_exp = weights[0].shape[1]
    width = weights[0].shape[3]
    tm = EXPERT_BLOCK
    w_spec = lambda: pl.BlockSpec((1, 1, k, tn), lambda j, e, *_: (layer, e, 0, j))
    return pl.pallas_call(
        functools.partial(_grouped_rows_kernel, n_weights=len(weights), tm=tm, tn=tn, tile_fn=tile_fn),
        grid_spec=pltpu.PrefetchScalarGridSpec(
            num_scalar_prefetch=3,
            grid=(width // tn, n_exp),
            in_specs=[pl.BlockSpec(memory_space=pl.ANY)] + [w_spec() for _ in weights],
            out_specs=pl.BlockSpec(memory_space=pl.ANY),
            scratch_shapes=[pltpu.VMEM((k, tn), BF16) for _ in weights] + [
                pltpu.VMEM((GROUPED_IN_SLOTS, GROUPED_UNIT_BLOCKS * tm, k), x.dtype),
                pltpu.VMEM((GROUPED_OUT_SLOTS, GROUPED_UNIT_BLOCKS * tm, tn), out_dtype),
                pltpu.VMEM((tm, tn), out_dtype),
                pltpu.SemaphoreType.DMA((GROUPED_IN_SLOTS,)),
                pltpu.SemaphoreType.DMA((GROUPED_OUT_SLOTS,)),
                pltpu.SemaphoreType.DMA]),
        out_shape=jax.ShapeDtypeStruct((rows, width), out_dtype),
        compiler_params=_cparams("arbitrary", "arbitrary"),
        name=name,
    )(start_blk, count_blk, spare_start, x, *weights)


def _swiglu_tile(x, w1_bf, w3_bf):
    xb = x.astype(BF16)
    a = jnp.dot(xb, w1_bf[...], preferred_element_type=F32)
    b = jnp.dot(xb, w3_bf[...], preferred_element_type=F32)
    return ((a * _sigmoid(a)) * b).astype(BF16)


def _matmul_tile(x, w_bf):
    return jnp.dot(x, w_bf[...], preferred_element_type=F32)


def moe_up(xs, w1, w3, layer, groups, tn=1024):
    return _grouped_rows(xs, (w1, w3), layer, groups, BF16, tn, _swiglu_tile, "moe_up")


def moe_down(act, w2, layer, groups, tn=512):
    return _grouped_rows(act, (w2,), layer, groups, F32, tn, _matmul_tile, "moe_down")


def _combine_kernel(dest_ref, ys_ref, route_ref, h_ref, g_ref, hout_ref, n_ref, buf0, buf1, sems, *, tm):
    base = pl.program_id(0) * (tm * TOP_K)

    def row_copy(buf, slot, r, d):
        return pltpu.make_async_copy(ys_ref.at[pl.ds(d, 1)], buf.at[pl.ds(r, 1)], sems.at[slot])

    def issue(r, carry):
        row_copy(buf0, 0, r, dest_ref[base + TOP_K * r]).start()
        row_copy(buf1, 1, r, dest_ref[base + TOP_K * r + 1]).start()
        return carry

    lax.fori_loop(0, tm, issue, 0, unroll=ROW_DMA_UNROLL)
    for _ in range(tm):
        row_copy(buf0, 0, 0, 0).wait()
    for _ in range(tm):
        row_copy(buf1, 1, 0, 0).wait()

    route = route_ref[...]
    g0 = route[:, ROUTE_G0:ROUTE_G0 + 1]
    g1 = route[:, ROUTE_G1:ROUTE_G1 + 1]
    hn = h_ref[...] + (g0 * buf0[...] + g1 * buf1[...])
    hout_ref[...] = hn
    n_ref[...] = (_rms(hn) * g_ref[...]).astype(n_ref.dtype)


def combine(ys, dest, route, h, g, norm_dtype, tm=256):
    n, d = h.shape
    row = lambda: pl.BlockSpec((tm, d), lambda i, dest: (i, 0))
    return pl.pallas_call(
        functools.partial(_combine_kernel, tm=tm),
        grid_spec=pltpu.PrefetchScalarGridSpec(
            num_scalar_prefetch=1,
            grid=(n // tm,),
            in_specs=[pl.BlockSpec(memory_space=pl.ANY),
                      pl.BlockSpec((tm, LANES), lambda i, dest: (i, 0)),
                      row(),
                      pl.BlockSpec((1, d), lambda i, dest: (0, 0))],
            out_specs=[row(), row()],
            scratch_shapes=[pltpu.VMEM((tm, d), F32),
                            pltpu.VMEM((tm, d), F32),
                            pltpu.SemaphoreType.DMA((TOP_K,))]),
        out_shape=[jax.ShapeDtypeStruct((n, d), F32), jax.ShapeDtypeStruct((n, d), norm_dtype)],
        compiler_params=_cparams("arbitrary"),
        name="moe_combine",
    )(dest, ys, route, h, g.reshape(1, d))


def moe_layer(hn_bf, hn32, h, w_router, w1, w3, w2, layer, g_next, norm_dtype):
    n = h.shape[0]
    eb = EXPERT_BLOCK
    route, cnt = router(hn_bf, w_router)
    counts = cnt[0, :N_EXPERTS].astype(jnp.int32)
    padded = (counts + eb - 1) // eb * eb
    pend = jnp.cumsum(padded)
    pstart = pend - padded
    experts = route[:, ROUTE_E0:ROUTE_E1 + 1].astype(jnp.int32)
    ranks = route[:, ROUTE_R0:ROUTE_R1 + 1].astype(jnp.int32)
    dest = (pstart[experts] + ranks).reshape(-1)
    n_blocks = -(-(n * TOP_K) // eb) + N_EXPERTS
    tail_start = jnp.where(padded > 0, pend - eb, -1)
    spare = pend[-1] + jnp.arange(N_EXPERTS) * eb
    spare_start = jnp.where(spare < n_blocks * eb, spare, -1).astype(jnp.int32)
    zero_start = jnp.concatenate([tail_start.astype(jnp.int32), spare_start])
    groups = ((pstart // eb).astype(jnp.int32), (padded // eb).astype(jnp.int32), spare_start)
    xs = dispatch(hn32, dest, zero_start, n_blocks * eb)
    act = moe_up(xs, w1, w3, layer, groups)
    ys = moe_down(act, w2, layer, groups)
    return combine(ys, dest, route, h, g_next, norm_dtype)


def kernel(x, norm_mix_g, norm_ffn_g, norm_final_g, w_in, rel_bias, conv_w, conv_b, lru_wa, lru_ba, lru_wx, lru_bx, lru_lambda, w_att_o, w_rec_o, w_out, ffn_w1, ffn_w3, ffn_w2, router_w, moe_w1, moe_w3, moe_w2):
    b, s, d = x.shape
    n = b * s
    depth = w_in.shape[0]
    aw = N_HEADS * HEAD_DIM
    lw = conv_w.shape[2]
    assert lw == aw and d % aw == 0, "column-block indexing assumes equal branch widths"
    h = x.reshape(n, d)
    xn = rmsnorm(h, norm_mix_g[0], BF16)
    out = None
    for l in range(depth):
        cols = in_projection(xn, w_in, l, aw, HEAD_DIM ** -0.5 * LOG2_E)
        att = chunked_attention(cols, _rel_bias_table(rel_bias[l]), s)
        w_gates = jnp.concatenate([lru_wa[l], lru_wx[l]], axis=-1).astype(BF16)
        rec = recurrent_branch(cols, conv_w[l], conv_b[l], w_gates, lru_ba[l], lru_bx[l], lru_lambda[l],
                               b, s, rx_col=3)
        merged = gated_merge(att, rec, w_att_o[l].astype(BF16), w_rec_o[l].astype(BF16), cols, ga_col=5)
        last = l == depth - 1
        g_next = norm_final_g if last else norm_mix_g[l + 1]
        next_dtype = F32 if last else BF16
        i = l // 2
        if l % 2 == 0:
            h, hn = proj_res_norm(merged, w_out[l].astype(BF16), h, norm_ffn_g[l], [BF16])
            act = ffn_up(hn, ffn_w1, ffn_w3, i)
            h, xn = proj_res_norm(act, ffn_w2[i].astype(BF16), h, g_next, [next_dtype])
        else:
            h, hn, hn32 = proj_res_norm(merged, w_out[l].astype(BF16), h, norm_ffn_g[l], [BF16, F32])
            h, xn = moe_layer(hn, hn32, h, router_w[i], moe_w1, moe_w3, moe_w2, i, g_next, next_dtype)
        out = xn
    return out.reshape(b, s, d)
```

```python
import functools

import numpy as np
import jax
import jax.numpy as jnp
from jax import lax
from jax.experimental import pallas as pl
from jax.experimental.pallas import tpu as pltpu

CHUNK = 64
LEFT_CHUNKS = 8
N_HEADS = 8
HEAD_DIM = 128
MAX_REL_DIST = 256
LRU_BLOCKS = 8
CONV_W = 4
LRU_C = 8.0
N_EXPERTS = 8
TOP_K = 2
EXPERT_BLOCK = 256
RMS_EPS = 1e-6
NEG_INF = -1e30
LOG2_E = float(np.log2(np.e))

LANES = 128
SUBLANES = 8
VMEM_LIMIT_BYTES = 56 * 1024 * 1024

ATT_QBLOCK = 4 * CHUNK
ATT_KBLOCKS = (LEFT_CHUNKS * CHUNK) // ATT_QBLOCK + 1

ROW_DMA_UNROLL = 8

GROUPED_UNIT_BLOCKS = 2
GROUPED_IN_SLOTS = 2
GROUPED_OUT_SLOTS = 2
MAX_PLAIN_DMA_BYTES = 2 * 1024 * 1024

F32 = jnp.float32
BF16 = jnp.bfloat16


def _cparams(*sem):
    return pltpu.CompilerParams(dimension_semantics=sem, vmem_limit_bytes=VMEM_LIMIT_BYTES)


def _resident(shape, index_map):
    return pl.BlockSpec(shape, index_map, pipeline_mode=pl.Buffered(1))


def _rms(x):
    return x * lax.rsqrt(jnp.mean(x * x, axis=-1, keepdims=True) + RMS_EPS)


def _sigmoid(x):
    return 1.0 / (1.0 + jnp.exp(-x))


def _gelu_tanh(x):
    c = np.float32(np.sqrt(2.0 / np.pi))
    return 0.5 * x * (1.0 + jnp.tanh(c * (x + 0.044715 * (x * x * x))))


def _softplus(x):
    return jnp.maximum(x, 0.0) + jnp.log1p(jnp.exp(-jnp.abs(x)))


def _rmsnorm_kernel(x_ref, g_ref, o_ref):
    o_ref[...] = (_rms(x_ref[...]) * g_ref[...]).astype(o_ref.dtype)


def rmsnorm(x, g, out_dtype, tm=512):
    n, d = x.shape
    return pl.pallas_call(
        _rmsnorm_kernel,
        grid=(n // tm,),
        in_specs=[pl.BlockSpec((tm, d), lambda i: (i, 0)),
                  pl.BlockSpec((1, d), lambda i: (0, 0))],
        out_specs=pl.BlockSpec((tm, d), lambda i: (i, 0)),
        out_shape=jax.ShapeDtypeStruct((n, d), out_dtype),
        compiler_params=_cparams("arbitrary"),
        name="rmsnorm",
    )(x, g.reshape(1, d))


def _inproj_kernel(x_ref, w_ref, o_ref, w_bf, *, q_scale):
    @pl.when(pl.program_id(1) == 0)
    def _():
        w_bf[...] = w_ref[0].astype(BF16)

    acc = jnp.dot(x_ref[...], w_bf[...], preferred_element_type=F32)
    s = jnp.where(pl.program_id(0) == 0, np.float32(q_scale), np.float32(1.0))
    o_ref[...] = (acc * s).astype(o_ref.dtype)


def in_projection(xn, w, layer, q_width, q_scale, tm=1024):
    n, d = xn.shape
    width = w.shape[2]
    tn = q_width
    return pl.pallas_call(
        functools.partial(_inproj_kernel, q_scale=q_scale),
        grid=(width // tn, n // tm),
        in_specs=[pl.BlockSpec((tm, d), lambda j, i: (i, 0)),
                  pl.BlockSpec((1, d, tn), lambda j, i: (layer, 0, j))],
        out_specs=pl.BlockSpec((tm, tn), lambda j, i: (i, j)),
        out_shape=jax.ShapeDtypeStruct((n, width), BF16),
        scratch_shapes=[pltpu.VMEM((d, tn), BF16)],
        compiler_params=_cparams("arbitrary", "arbitrary"),
        name="in_projection",
    )(xn, w)


def _rel_bias_table(rel_bias):
    qb = ATT_QBLOCK
    kw = ATT_KBLOCKS * ATT_QBLOCK
    n_heads, n_rel = rel_bias.shape
    period = qb + kw
    dist = LEFT_CHUNKS * CHUNK + (qb - 1) - np.arange(period)
    idx = np.clip(dist, -(CHUNK - 1), MAX_REL_DIST) + CHUNK - 1
    n_far = int(np.sum(idx == n_rel - 1)) - 1
    n_near = int(np.sum(idx == 0)) - 1
    assert np.array_equal(idx, np.concatenate([np.full(n_far, n_rel - 1), np.arange(n_rel)[::-1], np.full(n_near, 0)]))
    rb = rel_bias.astype(F32)
    diag = jnp.concatenate([jnp.broadcast_to(rb[:, -1:], (n_heads, n_far)), rb[:, ::-1],
                            jnp.broadcast_to(rb[:, :1], (n_heads, n_near))], axis=1)
    rep = jnp.broadcast_to(diag[:, None, :], (n_heads, qb, period)).reshape(n_heads, qb * period)
    skew = rep[:, :qb * (period - 1)].reshape(n_heads, qb, period - 1)
    tab = skew[:, :, qb - 1:qb - 1 + kw]
    qc = np.arange(qb)[:, None] // CHUNK
    kc = np.arange(kw)[None, :] // CHUNK
    valid = (kc >= qc) & (kc <= qc + LEFT_CHUNKS)
    return jnp.where(valid[None], tab * LOG2_E, NEG_INF)


def _attn_heads(q_ref, k_refs, v_refs, b_ref, o_ref, penalties):
    qb = ATT_QBLOCK
    nt = (((1,), (1,)), ((), ()))
    for h in range(N_HEADS):
        sl = slice(h * HEAD_DIM, (h + 1) * HEAD_DIM)
        q = q_ref[:, sl]
        s = []
        for c, k_ref in enumerate(k_refs):
            sc = lax.dot_general(q, k_ref[:, sl], nt, preferred_element_type=F32) + b_ref[h, :, c * qb:(c + 1) * qb]
            if penalties is not None and c < len(penalties):
                sc = sc + penalties[c]
            s.append(sc)
        m = jnp.max(functools.reduce(jnp.maximum, s), axis=-1, keepdims=True)
        p = [jnp.exp2(sc - m) for sc in s]
        l = jnp.sum(functools.reduce(jnp.add, p), axis=-1, keepdims=True)
        acc = None
        for pc, v_ref in zip(p, v_refs):
            pv = jnp.dot(pc.astype(BF16), v_ref[:, sl], preferred_element_type=F32)
            acc = pv if acc is None else acc + pv
        o_ref[:, sl] = (acc / l).astype(o_ref.dtype)


def _attn_kernel(q_ref, k0_ref, k1_ref, k2_ref, v0_ref, v1_ref, v2_ref, b_ref, o_ref, *, blocks_per_seq):
    i = pl.program_id(0) % blocks_per_seq
    k_refs = (k0_ref, k1_ref, k2_ref)
    v_refs = (v0_ref, v1_ref, v2_ref)
    first = i < ATT_KBLOCKS - 1

    @pl.when(first)
    def _():
        pen = [jnp.where(i >= ATT_KBLOCKS - 1 - c, np.float32(0.0), np.float32(NEG_INF))
               for c in range(ATT_KBLOCKS - 1)]
        _attn_heads(q_ref, k_refs, v_refs, b_ref, o_ref, pen)

    @pl.when(jnp.logical_not(first))
    def _():
        _attn_heads(q_ref, k_refs, v_refs, b_ref, o_ref, None)


def chunked_attention(cols, bias_tab, seq):
    n = cols.shape[0]
    qb = ATT_QBLOCK
    aw = N_HEADS * HEAD_DIM
    bps = seq // qb
    assert ATT_KBLOCKS == 3 and seq % qb == 0

    def kv_spec(col, back):
        return pl.BlockSpec((qb, aw), lambda g: (g - jnp.minimum(g % bps, back), col))

    return pl.pallas_call(
        functools.partial(_attn_kernel, blocks_per_seq=bps),
        grid=(n // qb,),
        in_specs=[pl.BlockSpec((qb, aw), lambda g: (g, 0)),
                  kv_spec(1, 2), kv_spec(1, 1), kv_spec(1, 0),
                  kv_spec(2, 2), kv_spec(2, 1), kv_spec(2, 0),
                  _resident(bias_tab.shape, lambda g: (0, 0, 0))],
        out_specs=pl.BlockSpec((qb, aw), lambda g: (g, 0)),
        out_shape=jax.ShapeDtypeStruct((n, aw), BF16),
        compiler_params=_cparams("arbitrary"),
        name="chunked_attention",
    )(cols, cols, cols, cols, cols, cols, cols, bias_tab)


def _rec_kernel(rx_ref, ry_ref, cw_ref, cb_ref, wg_ref, ba_ref, bx_ref, lam_ref, o_ref,
                xbuf, a_s, u_s, carry, *, rows, width, bw):
    t = pl.program_id(1)
    pad = SUBLANES

    @pl.when(t == 0)
    def _():
        xbuf[0:pad, :] = jnp.zeros((pad, width), F32)
        carry[...] = jnp.zeros((pad, width), F32)

    x = rx_ref[...].astype(F32)
    xbuf[pad:pad + rows, :] = x
    rc = cb_ref[...] + cw_ref[0:1, :] * xbuf[pad - 3:pad - 3 + rows, :]
    rc = rc + cw_ref[1:2, :] * xbuf[pad - 2:pad - 2 + rows, :]
    rc = rc + cw_ref[2:3, :] * xbuf[pad - 1:pad - 1 + rows, :]
    rc = rc + cw_ref[3:4, :] * x
    xbuf[0:pad, :] = xbuf[rows:rows + pad, :]
    u_s[...] = rc

    neg_c_sp = -LRU_C * _softplus(-lam_ref[...])
    for nb in range(width // bw):
        sl = slice(nb * bw, (nb + 1) * bw)
        rcn = u_s[:, sl]
        g = jnp.dot(rcn.astype(BF16), wg_ref[nb], preferred_element_type=F32)
        r = _sigmoid(g[:, :bw] + ba_ref[:, sl])
        ig = _sigmoid(g[:, bw:] + bx_ref[:, sl])
        log_a = neg_c_sp[:, sl] * r
        a_s[:, sl] = jnp.exp(log_a)
        th = jnp.tanh(log_a)
        one_m_a2 = -2.0 * th / (1.0 - th)
        root = one_m_a2 * lax.rsqrt(jnp.maximum(one_m_a2, np.float32(1e-30)))
        u_s[:, sl] = root * (ig * rcn)

    row = lax.broadcasted_iota(jnp.int32, (pad, width), 0)

    def group(gi, c):
        r0 = pl.multiple_of(gi * pad, pad)
        a = a_s[pl.ds(r0, pad), :]
        h = u_s[pl.ds(r0, pad), :]
        for s in (1, 2, 4):
            keep = row >= s
            a_prev = jnp.where(keep, pltpu.roll(a, s, 0), 1.0)
            h_prev = jnp.where(keep, pltpu.roll(h, s, 0), 0.0)
            h = a * h_prev + h
            a = a * a_prev
        h = h + a * c
        u_s[pl.ds(r0, pad), :] = h
        return jnp.broadcast_to(h[pad - 1:pad, :], (pad, width))

    carry[...] = lax.fori_loop(0, rows // pad, group, carry[...])
    o_ref[...] = (_gelu_tanh(ry_ref[...].astype(F32)) * u_s[...]).astype(o_ref.dtype)


def recurrent_branch(cols, conv_w, conv_b, w_gates, b_a, b_x, lam, batch, seq, rx_col, rows=256):
    n = cols.shape[0]
    width = conv_w.shape[1]
    bw = width // LRU_BLOCKS
    spb = seq // rows
    vec = lambda: pl.BlockSpec((1, width), lambda b, t: (0, 0))
    return pl.pallas_call(
        functools.partial(_rec_kernel, rows=rows, width=width, bw=bw),
        grid=(batch, spb),
        in_specs=[pl.BlockSpec((rows, width), lambda b, t: (b * spb + t, rx_col)),
                  pl.BlockSpec((rows, width), lambda b, t: (b * spb + t, rx_col + 1)),
                  pl.BlockSpec((CONV_W, width), lambda b, t: (0, 0)),
                  vec(),
                  pl.BlockSpec((LRU_BLOCKS, bw, 2 * bw), lambda b, t: (0, 0, 0)),
                  vec(), vec(), vec()],
        out_specs=pl.BlockSpec((rows, width), lambda b, t: (b * spb + t, 0)),
        out_shape=jax.ShapeDtypeStruct((n, width), BF16),
        scratch_shapes=[pltpu.VMEM((rows + SUBLANES, width), F32),
                        pltpu.VMEM((rows, width), F32),
                        pltpu.VMEM((rows, width), F32),
                        pltpu.VMEM((SUBLANES, width), F32)],
        compiler_params=_cparams("arbitrary", "arbitrary"),
        name="recurrent_branch",
    )(cols, cols, conv_w, conv_b.reshape(1, width), w_gates,
      b_a.reshape(1, width), b_x.reshape(1, width), lam.reshape(1, width))


def _proj_res_norm_kernel(x_ref, w_ref, h_ref, g_ref, hout_ref, n_ref, *extra):
    hn = h_ref[...] + jnp.dot(x_ref[...], w_ref[...], preferred_element_type=F32)
    hout_ref[...] = hn
    nrm = _rms(hn) * g_ref[...]
    n_ref[...] = nrm.astype(n_ref.dtype)
    for r in extra:
        r[...] = nrm.astype(r.dtype)


def proj_res_norm(x, w, h, g, norm_dtypes, tm=256):
    n, k = x.shape
    d = w.shape[1]
    row = lambda: pl.BlockSpec((tm, d), lambda i: (i, 0))
    return pl.pallas_call(
        _proj_res_norm_kernel,
        grid=(n // tm,),
        in_specs=[pl.BlockSpec((tm, k), lambda i: (i, 0)),
                  _resident((k, d), lambda i: (0, 0)),
                  row(),
                  pl.BlockSpec((1, d), lambda i: (0, 0))],
        out_specs=[row()] + [row() for _ in norm_dtypes],
        out_shape=[jax.ShapeDtypeStruct((n, d), F32)] + [jax.ShapeDtypeStruct((n, d), dt) for dt in norm_dtypes],
        compiler_params=_cparams("arbitrary"),
        name="proj_res_norm",
    )(x, w, h, g.reshape(1, d))


def _mix_out_kernel(*refs, n_gate_blocks):
    att_ref, rec_ref = refs[:2]
    ga_refs = refs[2:2 + n_gate_blocks]
    gb_refs = refs[2 + n_gate_blocks:2 + 2 * n_gate_blocks]
    wa_ref, wr_ref, wo_ref, h_ref, g_ref, hout_ref = refs[2 + 2 * n_gate_blocks:8 + 2 * n_gate_blocks]
    norm_refs = refs[8 + 2 * n_gate_blocks:]
    ya = jnp.dot(att_ref[...], wa_ref[...], preferred_element_type=F32)
    yr = jnp.dot(rec_ref[...], wr_ref[...], preferred_element_type=F32)
    ga = jnp.concatenate([r[...] for r in ga_refs], axis=1).astype(F32)
    gb = jnp.concatenate([r[...] for r in gb_refs], axis=1).astype(F32)
    merged = (_sigmoid(ga) * ya + _sigmoid(gb) * yr).astype(BF16)
    hn = h_ref[...] + jnp.dot(merged, wo_ref[...], preferred_element_type=F32)
    hout_ref[...] = hn
    nrm = _rms(hn) * g_ref[...]
    for r in norm_refs:
        r[...] = nrm.astype(r.dtype)


def mix_out(att, rec, cols, ga_col, w_att_o, w_rec_o, w_out, h, g, norm_dtypes, tm=256):
    n, aw = att.shape
    rw = rec.shape[1]
    d = w_out.shape[1]
    ngb = d // aw
    row = lambda: pl.BlockSpec((tm, d), lambda i: (i, 0))
    gate = lambda c: pl.BlockSpec((tm, aw), lambda i: (i, c))
    return pl.pallas_call(
        functools.partial(_mix_out_kernel, n_gate_blocks=ngb),
        grid=(n // tm,),
        in_specs=[pl.BlockSpec((tm, aw), lambda i: (i, 0)),
                  pl.BlockSpec((tm, rw), lambda i: (i, 0))]
                 + [gate(ga_col + c) for c in range(ngb)]
                 + [gate(ga_col + ngb + c) for c in range(ngb)]
                 + [_resident((aw, d), lambda i: (0, 0)),
                    _resident((rw, d), lambda i: (0, 0)),
                    _resident((d, d), lambda i: (0, 0)),
                    row(),
                    pl.BlockSpec((1, d), lambda i: (0, 0))],
        out_specs=[row()] + [row() for _ in norm_dtypes],
        out_shape=[jax.ShapeDtypeStruct((n, d), F32)] + [jax.ShapeDtypeStruct((n, d), dt) for dt in norm_dtypes],
        compiler_params=_cparams("arbitrary"),
        name="mix_out",
    )(att, rec, *([cols] * (2 * ngb)), w_att_o, w_rec_o, w_out, h, g.reshape(1, d))


def _ffn_up_kernel(x_ref, w1_ref, w3_ref, o_ref, w1_bf, w3_bf):
    @pl.when(pl.program_id(1) == 0)
    def _():
        w1_bf[...] = w1_ref[0].astype(BF16)
        w3_bf[...] = w3_ref[0].astype(BF16)

    x = x_ref[...]
    a = jnp.dot(x, w1_bf[...], preferred_element_type=F32)
    b = jnp.dot(x, w3_bf[...], preferred_element_type=F32)
    o_ref[...] = ((a * _sigmoid(a)) * b).astype(o_ref.dtype)


def ffn_up(xn, w1, w3, layer, tm=1024, tn=512):
    n, d = xn.shape
    f = w1.shape[2]
    w_spec = lambda: pl.BlockSpec((1, d, tn), lambda j, i: (layer, 0, j))
    return pl.pallas_call(
        _ffn_up_kernel,
        grid=(f // tn, n // tm),
        in_specs=[pl.BlockSpec((tm, d), lambda j, i: (i, 0)), w_spec(), w_spec()],
        out_specs=pl.BlockSpec((tm, tn), lambda j, i: (i, j)),
        out_shape=jax.ShapeDtypeStruct((n, f), BF16),
        scratch_shapes=[pltpu.VMEM((d, tn), BF16), pltpu.VMEM((d, tn), BF16)],
        compiler_params=_cparams("arbitrary", "arbitrary"),
        name="ffn_up",
    )(xn, w1, w3)


ROUTE_E0, ROUTE_E1, ROUTE_G0, ROUTE_G1, ROUTE_R0, ROUTE_R1 = range(6)


def _router_kernel(x_ref, w_ref, tri_ref, route_ref, cnt_ref, carry):
    @pl.when(pl.program_id(0) == 0)
    def _():
        carry[...] = jnp.zeros_like(carry)

    tm = x_ref.shape[0]
    logits = jnp.dot(x_ref[...], w_ref[...], preferred_element_type=F32)
    lane = lax.broadcasted_iota(jnp.int32, (tm, LANES), 1).astype(F32)
    neg = np.float32(-np.inf)
    lg = jnp.where(lane < N_EXPERTS, logits, neg)
    m0 = jnp.max(lg, axis=-1, keepdims=True)
    e0 = jnp.min(jnp.where(lg == m0, lane, np.float32(LANES)), axis=-1, keepdims=True)
    lg1 = jnp.where(lane == e0, neg, lg)
    m1 = jnp.max(lg1, axis=-1, keepdims=True)
    e1 = jnp.min(jnp.where(lg1 == m1, lane, np.float32(LANES)), axis=-1, keepdims=True)
    ex = jnp.exp(m1 - m0)
    g0 = 1.0 / (1.0 + ex)
    g1 = ex / (1.0 + ex)
    sel0 = lane == e0
    sel1 = lane == e1
    member = jnp.where(sel0 | sel1, np.float32(1.0), np.float32(0.0))
    incl = jnp.dot(tri_ref[...], member.astype(BF16), preferred_element_type=F32)
    before = carry[...] + incl - member
    r0 = jnp.sum(jnp.where(sel0, before, 0.0), axis=-1, keepdims=True)
    r1 = jnp.sum(jnp.where(sel1, before, 0.0), axis=-1, keepdims=True)
    total = carry[...] + incl[tm - 1:tm, :]
    carry[...] = total
    cnt_ref[...] = jnp.broadcast_to(total, cnt_ref.shape)
    out = jnp.zeros((tm, LANES), F32)
    for k, v in ((ROUTE_E0, e0), (ROUTE_E1, e1), (ROUTE_G0, g0), (ROUTE_G1, g1), (ROUTE_R0, r0), (ROUTE_R1, r1)):
        out = jnp.where(lane == k, v, out)
    route_ref[...] = out


def router(hn, w_router, tm=512):
    n, d = hn.shape
    wr = jnp.zeros((d, LANES), BF16).at[:, :N_EXPERTS].set(w_router.astype(BF16))
    tri = jnp.asarray(np.tril(np.ones((tm, tm), np.float32)), BF16)
    return pl.pallas_call(
        _router_kernel,
        grid=(n // tm,),
        in_specs=[pl.BlockSpec((tm, d), lambda i: (i, 0)),
                  pl.BlockSpec((d, LANES), lambda i: (0, 0)),
                  pl.BlockSpec((tm, tm), lambda i: (0, 0))],
        out_specs=[pl.BlockSpec((tm, LANES), lambda i: (i, 0)),
                   pl.BlockSpec((SUBLANES, LANES), lambda i: (0, 0))],
        out_shape=[jax.ShapeDtypeStruct((n, LANES), F32),
                   jax.ShapeDtypeStruct((SUBLANES, LANES), F32)],
        scratch_shapes=[pltpu.VMEM((1, LANES), F32)],
        compiler_params=_cparams("arbitrary"),
        name="router",
    )(hn, wr, tri)


def _dispatch_kernel(dest_ref, zero_ref, x_ref, xs_ref, zbuf, sem, zsem, *, tm):
    step = pl.program_id(0)
    eb = EXPERT_BLOCK

    @pl.when(step == 0)
    def _():
        zbuf[...] = jnp.zeros_like(zbuf)
        for z in range(zero_ref.shape[0]):
            @pl.when(zero_ref[z] >= 0)
            def _():
                cp = pltpu.make_async_copy(zbuf, xs_ref.at[pl.ds(pl.multiple_of(zero_ref[z], eb), eb)], zsem)
                cp.start()
                cp.wait()

    base = step * (tm * TOP_K)

    def row_copy(r, d):
        return pltpu.make_async_copy(x_ref.at[pl.ds(r, 1)], xs_ref.at[pl.ds(d, 1)], sem)

    def issue(r, carry):
        row_copy(r, dest_ref[base + TOP_K * r]).start()
        row_copy(r, dest_ref[base + TOP_K * r + 1]).start()
        return carry

    lax.fori_loop(0, tm, issue, 0, unroll=ROW_DMA_UNROLL)
    for _ in range(TOP_K * tm):
        row_copy(0, 0).wait()


def dispatch(hn32, dest, zero_start, n_rows, tm=256):
    n, d = hn32.shape
    return pl.pallas_call(
        functools.partial(_dispatch_kernel, tm=tm),
        grid_spec=pltpu.PrefetchScalarGridSpec(
            num_scalar_prefetch=2,
            grid=(n // tm,),
            in_specs=[pl.BlockSpec((tm, d), lambda i, dest, tail: (i, 0))],
            out_specs=pl.BlockSpec(memory_space=pl.ANY),
            scratch_shapes=[pltpu.VMEM((EXPERT_BLOCK, d), F32),
                            pltpu.SemaphoreType.DMA,
                            pltpu.SemaphoreType.DMA]),
        out_shape=jax.ShapeDtypeStruct((n_rows, d), F32),
        compiler_params=_cparams("arbitrary"),
        name="moe_dispatch",
    )(dest, zero_start, hn32)


def _grouped_rows_kernel(start_ref, count_ref, spare_ref, x_hbm, *refs, n_weights, tm, tn, tile_fn):
    w_refs = refs[:n_weights]
    o_hbm = refs[n_weights]
    w_bf = refs[n_weights + 1:2 * n_weights + 1]
    xbuf, obuf, zbuf, xsem, osem, zsem = refs[2 * n_weights + 1:]
    j = pl.program_id(0)
    e = pl.program_id(1)
    nb = count_ref[e]
    first = start_ref[e]
    col0 = pl.multiple_of(j * tn, tn)

    n_in = xbuf.shape[0]
    n_out = obuf.shape[0]
    ub = xbuf.shape[1] // tm
    n_units = lax.div(nb + (ub - 1), ub)
    n_piece = -(-(tm * xbuf.shape[2] * xbuf.dtype.itemsize) // MAX_PLAIN_DMA_BYTES)
    piece = tm // n_piece

    def x_copies(u, b):
        slot = lax.rem(u, n_in)
        r0 = (first + u * ub + b) * tm
        return [pltpu.make_async_copy(x_hbm.at[pl.ds(pl.multiple_of(r0 + p * piece, piece), piece)],
                                      xbuf.at[slot, pl.ds(b * tm + p * piece, piece)], xsem.at[slot])
                for p in range(n_piece)]

    def o_copy(u, b):
        slot = lax.rem(u, n_out)
        rows = pl.ds(pl.multiple_of((first + u * ub + b) * tm, tm), tm)
        return pltpu.make_async_copy(obuf.at[slot, pl.ds(b * tm, tm)], o_hbm.at[rows, pl.ds(col0, tn)],
                                     osem.at[slot])

    def for_blocks(u, fn):
        fn(0)
        for b in range(1, ub):
            @pl.when(u * ub + b < nb)
            def _():
                fn(b)

    def x_start(u):
        for_blocks(u, lambda b: [cp.start(priority=1) for cp in x_copies(u, b)])

    def x_wait(u):
        for_blocks(u, lambda b: [cp.wait() for cp in x_copies(u, b)])

    @pl.when(nb > 0)
    def _():
        for ahead in range(n_in - 1):
            @pl.when(ahead < n_units)
            def _():
                x_start(ahead)
        for wb, w in zip(w_bf, w_refs):
            wb[...] = w[0, 0].astype(BF16)

        def unit(u, carry):
            x_wait(u)

            @pl.when(u + (n_in - 1) < n_units)
            def _():
                x_start(u + (n_in - 1))

            @pl.when(u >= n_out)
            def _():
                for_blocks(u - n_out, lambda b: o_copy(u - n_out, b).wait())

            slot_in = lax.rem(u, n_in)
            slot_out = lax.rem(u, n_out)
            have = nb - u * ub

            @pl.when(have >= ub)
            def _():
                obuf[slot_out] = tile_fn(xbuf[slot_in], *w_bf)

            for short in range(1, ub):
                @pl.when(have == short)
                def _():
                    obuf[slot_out, :short * tm] = tile_fn(xbuf[slot_in, :short * tm], *w_bf)

            for_blocks(u, lambda b: o_copy(u, b).start())
            return carry

        lax.fori_loop(0, n_units, unit, 0)

        for back in range(n_out, 0, -1):
            @pl.when(n_units >= back)
            def _():
                for_blocks(n_units - back, lambda b: o_copy(n_units - back, b).wait())

    @pl.when(e == pl.num_programs(1) - 1)
    def _():
        zbuf[...] = jnp.zeros_like(zbuf)

        def z_copy(z):
            rows = pl.ds(pl.multiple_of(spare_ref[z], tm), tm)
            return pltpu.make_async_copy(zbuf, o_hbm.at[rows, pl.ds(col0, tn)], zsem)

        for z in range(spare_ref.shape[0]):
            @pl.when(spare_ref[z] >= 0)
            def _():
                z_copy(z).start()
        for z in range(spare_ref.shape[0]):
            @pl.when(spare_ref[z] >= 0)
            def _():
                z_copy(z).wait()


def _grouped_rows(x, weights, layer, groups, out_dtype, tn, tile_fn, name):
    start_blk, count_blk, spare_start = groups
    rows, k = x.shape
    n_exp = weights[0].shape[1]
    width = weights[0].shape[3]
    tm = EXPERT_BLOCK
    w_spec = lambda: pl.BlockSpec((1, 1, k, tn), lambda j, e, *_: (layer, e, 0, j))
    return pl.pallas_call(
        functools.partial(_grouped_rows_kernel, n_weights=len(weights), tm=tm, tn=tn, tile_fn=tile_fn),
        grid_spec=pltpu.PrefetchScalarGridSpec(
            num_scalar_prefetch=3,
            grid=(width // tn, n_exp),
            in_specs=[pl.BlockSpec(memory_space=pl.ANY)] + [w_spec() for _ in weights],
            out_specs=pl.BlockSpec(memory_space=pl.ANY),
            scratch_shapes=[pltpu.VMEM((k, tn), BF16) for _ in weights] + [
                pltpu.VMEM((GROUPED_IN_SLOTS, GROUPED_UNIT_BLOCKS * tm, k), x.dtype),
                pltpu.VMEM((GROUPED_OUT_SLOTS, GROUPED_UNIT_BLOCKS * tm, tn), out_dtype),
                pltpu.VMEM((tm, tn), out_dtype),
                pltpu.SemaphoreType.DMA((GROUPED_IN_SLOTS,)),
                pltpu.SemaphoreType.DMA((GROUPED_OUT_SLOTS,)),
                pltpu.SemaphoreType.DMA]),
        out_shape=jax.ShapeDtypeStruct((rows, width), out_dtype),
        compiler_params=_cparams("arbitrary", "arbitrary"),
        name=name,
    )(start_blk, count_blk, spare_start, x, *weights)


def _swiglu_tile(x, w1_bf, w3_bf):
    xb = x.astype(BF16)
    a = jnp.dot(xb, w1_bf[...], preferred_element_type=F32)
    b = jnp.dot(xb, w3_bf[...], preferred_element_type=F32)
    return ((a * _sigmoid(a)) * b).astype(BF16)


def _matmul_tile(x, w_bf):
    return jnp.dot(x, w_bf[...], preferred_element_type=F32)


def moe_up(xs, w1, w3, layer, groups, tn=1024):
    return _grouped_rows(xs, (w1, w3), layer, groups, BF16, tn, _swiglu_tile, "moe_up")


def moe_down(act, w2, layer, groups, tn=512):
    return _grouped_rows(act, (w2,), layer, groups, F32, tn, _matmul_tile, "moe_down")


def _combine_kernel(dest_ref, ys_ref, route_ref, h_ref, g_ref, hout_ref, n_ref, bufs, sems, *, tm):
    step = pl.program_id(0)
    n_steps = pl.num_programs(0)
    slot = lax.rem(step, 2)

    def row_copy(k, s, r, d):
        return pltpu.make_async_copy(ys_ref.at[pl.ds(d, 1)], bufs.at[k, s, pl.ds(r, 1)], sems.at[k, s])

    def gather(st, s):
        base = st * (tm * TOP_K)

        def issue(r, carry):
            for k in range(TOP_K):
                row_copy(k, s, r, dest_ref[base + TOP_K * r + k]).start()
            return carry

        lax.fori_loop(0, tm, issue, 0, unroll=ROW_DMA_UNROLL)

    @pl.when(step == 0)
    def _():
        gather(0, 0)

    @pl.when(step + 1 < n_steps)
    def _():
        gather(step + 1, 1 - slot)

    for k in range(TOP_K):
        for _ in range(tm):
            row_copy(k, slot, 0, 0).wait()

    route = route_ref[...]
    g0 = route[:, ROUTE_G0:ROUTE_G0 + 1]
    g1 = route[:, ROUTE_G1:ROUTE_G1 + 1]
    hn = h_ref[...] + (g0 * bufs[0, slot] + g1 * bufs[1, slot])
    hout_ref[...] = hn
    n_ref[...] = (_rms(hn) * g_ref[...]).astype(n_ref.dtype)


def combine(ys, dest, route, h, g, norm_dtype, tm=256):
    n, d = h.shape
    row = lambda: pl.BlockSpec((tm, d), lambda i, dest: (i, 0))
    return pl.pallas_call(
        functools.partial(_combine_kernel, tm=tm),
        grid_spec=pltpu.PrefetchScalarGridSpec(
            num_scalar_prefetch=1,
            grid=(n // tm,),
            in_specs=[pl.BlockSpec(memory_space=pl.ANY),
                      pl.BlockSpec((tm, LANES), lambda i, dest: (i, 0)),
                      row(),
                      pl.BlockSpec((1, d), lambda i, dest: (0, 0))],
            out_specs=[row(), row()],
            scratch_shapes=[pltpu.VMEM((TOP_K, 2, tm, d), F32),
                            pltpu.SemaphoreType.DMA((TOP_K, 2))]),
        out_shape=[jax.ShapeDtypeStruct((n, d), F32), jax.ShapeDtypeStruct((n, d), norm_dtype)],
        compiler_params=_cparams("arbitrary"),
        name="moe_combine",
    )(dest, ys, route, h, g.reshape(1, d))


def moe_layer(hn_bf, hn32, h, w_router, w1, w3, w2, layer, g_next, norm_dtype):
    n = h.shape[0]
    eb = EXPERT_BLOCK
    route, cnt = router(hn_bf, w_router)
    counts = cnt[0, :N_EXPERTS].astype(jnp.int32)
    padded = (counts + eb - 1) // eb * eb
    pend = jnp.cumsum(padded)
    pstart = pend - padded
    experts = route[:, ROUTE_E0:ROUTE_E1 + 1].astype(jnp.int32)
    ranks = route[:, ROUTE_R0:ROUTE_R1 + 1].astype(jnp.int32)
    dest = (pstart[experts] + ranks).reshape(-1)
    n_blocks = -(-(n * TOP_K) // eb) + N_EXPERTS
    tail_start = jnp.where(padded > 0, pend - eb, -1)
    spare = pend[-1] + jnp.arange(N_EXPERTS) * eb
    spare_start = jnp.where(spare < n_blocks * eb, spare, -1).astype(jnp.int32)
    zero_start = jnp.concatenate([tail_start.astype(jnp.int32), spare_start])
    groups = ((pstart // eb).astype(jnp.int32), (padded // eb).astype(jnp.int32), spare_start)
    xs = dispatch(hn32, dest, zero_start, n_blocks * eb)
    act = moe_up(xs, w1, w3, layer, groups)
    ys = moe_down(act, w2, layer, groups)
    return combine(ys, dest, route, h, g_next, norm_dtype)


def kernel(x, norm_mix_g, norm_ffn_g, norm_final_g, w_in, rel_bias, conv_w, conv_b, lru_wa, lru_ba, lru_wx, lru_bx, lru_lambda, w_att_o, w_rec_o, w_out, ffn_w1, ffn_w3, ffn_w2, router_w, moe_w1, moe_w3, moe_w2):
    b, s, d = x.shape
    n = b * s
    depth = w_in.shape[0]
    aw = N_HEADS * HEAD_DIM
    lw = conv_w.shape[2]
    assert lw == aw and d % aw == 0, "column-block indexing assumes equal branch widths"
    h = x.reshape(n, d)
    xn = rmsnorm(h, norm_mix_g[0], BF16)
    out = None
    for l in range(depth):
        cols = in_projection(xn, w_in, l, aw, HEAD_DIM ** -0.5 * LOG2_E)
        att = chunked_attention(cols, _rel_bias_table(rel_bias[l]), s)
        w_gates = jnp.concatenate([lru_wa[l], lru_wx[l]], axis=-1).astype(BF16)
        rec = recurrent_branch(cols, conv_w[l], conv_b[l], w_gates, lru_ba[l], lru_bx[l], lru_lambda[l],
                               b, s, rx_col=3)
        last = l == depth - 1
        g_next = norm_final_g if last else norm_mix_g[l + 1]
        next_dtype = F32 if last else BF16
        i = l // 2
        dense = l % 2 == 0
        h, hn, *hn32 = mix_out(att, rec, cols, 5, w_att_o[l].astype(BF16), w_rec_o[l].astype(BF16),
                               w_out[l].astype(BF16), h, norm_ffn_g[l], [BF16] if dense else [BF16, F32])
        if dense:
            act = ffn_up(hn, ffn_w1, ffn_w3, i)
            h, xn = proj_res_norm(act, ffn_w2[i].astype(BF16), h, g_next, [next_dtype])
        else:
            h, xn = moe_layer(hn, hn32[0], h, router_w[i], moe_w1, moe_w3, moe_w2, i, g_next, next_dtype)
        out = xn
    return out.reshape(b, s, d)
```

```python
import functools

import numpy as np
import jax
import jax.numpy as jnp
from jax import lax
from jax.experimental import pallas as pl
from jax.experimental.pallas import tpu as pltpu

CHUNK = 64
LEFT_CHUNKS = 8
N_HEADS = 8
HEAD_DIM = 128
MAX_REL_DIST = 256
LRU_BLOCKS = 8
CONV_W = 4
LRU_C = 8.0
N_EXPERTS = 8
TOP_K = 2
EXPERT_BLOCK = 256
RMS_EPS = 1e-6
NEG_INF = -1e30
LOG2_E = float(np.log2(np.e))

LANES = 128
SUBLANES = 8
VMEM_LIMIT_BYTES = 56 * 1024 * 1024

ATT_QBLOCK = 4 * CHUNK
ATT_KBLOCKS = (LEFT_CHUNKS * CHUNK) // ATT_QBLOCK + 1

ROW_DMA_UNROLL = 8

GROUPED_UNIT_BLOCKS = 2
GROUPED_IN_SLOTS = 2
GROUPED_OUT_SLOTS = 2
MAX_PLAIN_DMA_BYTES = 2 * 1024 * 1024

F32 = jnp.float32
BF16 = jnp.bfloat16


def _cparams(*sem):
    return pltpu.CompilerParams(dimension_semantics=sem, vmem_limit_bytes=VMEM_LIMIT_BYTES)


def _resident(shape, index_map):
    return pl.BlockSpec(shape, index_map, pipeline_mode=pl.Buffered(1))


def _rms(x):
    return x * lax.rsqrt(jnp.mean(x * x, axis=-1, keepdims=True) + RMS_EPS)


def _sigmoid(x):
    return 1.0 / (1.0 + jnp.exp(-x))


def _gelu_tanh(x):
    c = np.float32(np.sqrt(2.0 / np.pi))
    return 0.5 * x * (1.0 + jnp.tanh(c * (x + 0.044715 * (x * x * x))))


def _softplus(x):
    return jnp.maximum(x, 0.0) + jnp.log1p(jnp.exp(-jnp.abs(x)))


def _rmsnorm_kernel(x_ref, g_ref, o_ref):
    o_ref[...] = (_rms(x_ref[...]) * g_ref[...]).astype(o_ref.dtype)


def rmsnorm(x, g, out_dtype, tm=512):
    n, d = x.shape
    return pl.pallas_call(
        _rmsnorm_kernel,
        grid=(n // tm,),
        in_specs=[pl.BlockSpec((tm, d), lambda i: (i, 0)),
                  pl.BlockSpec((1, d), lambda i: (0, 0))],
        out_specs=pl.BlockSpec((tm, d), lambda i: (i, 0)),
        out_shape=jax.ShapeDtypeStruct((n, d), out_dtype),
        compiler_params=_cparams("arbitrary"),
        name="rmsnorm",
    )(x, g.reshape(1, d))


def _inproj_kernel(x_ref, w_ref, o_ref, w_bf, *, q_scale):
    @pl.when(pl.program_id(1) == 0)
    def _():
        w_bf[...] = w_ref[0].astype(BF16)

    acc = jnp.dot(x_ref[...], w_bf[...], preferred_element_type=F32)
    s = jnp.where(pl.program_id(0) == 0, np.float32(q_scale), np.float32(1.0))
    o_ref[...] = (acc * s).astype(o_ref.dtype)


def in_projection(xn, w, layer, q_width, q_scale, tm=1024):
    n, d = xn.shape
    width = w.shape[2]
    tn = q_width
    return pl.pallas_call(
        functools.partial(_inproj_kernel, q_scale=q_scale),
        grid=(width // tn, n // tm),
        in_specs=[pl.BlockSpec((tm, d), lambda j, i: (i, 0)),
                  pl.BlockSpec((1, d, tn), lambda j, i: (layer, 0, j))],
        out_specs=pl.BlockSpec((tm, tn), lambda j, i: (i, j)),
        out_shape=jax.ShapeDtypeStruct((n, width), BF16),
        scratch_shapes=[pltpu.VMEM((d, tn), BF16)],
        compiler_params=_cparams("arbitrary", "arbitrary"),
        name="in_projection",
    )(xn, w)


def _rel_bias_table(rel_bias):
    qb = ATT_QBLOCK
    kw = ATT_KBLOCKS * ATT_QBLOCK
    n_heads, n_rel = rel_bias.shape
    period = qb + kw
    dist = LEFT_CHUNKS * CHUNK + (qb - 1) - np.arange(period)
    idx = np.clip(dist, -(CHUNK - 1), MAX_REL_DIST) + CHUNK - 1
    n_far = int(np.sum(idx == n_rel - 1)) - 1
    n_near = int(np.sum(idx == 0)) - 1
    assert np.array_equal(idx, np.concatenate([np.full(n_far, n_rel - 1), np.arange(n_rel)[::-1], np.full(n_near, 0)]))
    rb = rel_bias.astype(F32)
    diag = jnp.concatenate([jnp.broadcast_to(rb[:, -1:], (n_heads, n_far)), rb[:, ::-1],
                            jnp.broadcast_to(rb[:, :1], (n_heads, n_near))], axis=1)
    rep = jnp.broadcast_to(diag[:, None, :], (n_heads, qb, period)).reshape(n_heads, qb * period)
    skew = rep[:, :qb * (period - 1)].reshape(n_heads, qb, period - 1)
    tab = skew[:, :, qb - 1:qb - 1 + kw]
    qc = np.arange(qb)[:, None] // CHUNK
    kc = np.arange(kw)[None, :] // CHUNK
    valid = (kc >= qc) & (kc <= qc + LEFT_CHUNKS)
    return jnp.where(valid[None], tab * LOG2_E, NEG_INF)


def _attn_heads(q_ref, k_refs, v_refs, b_ref, o_ref, penalties):
    qb = ATT_QBLOCK
    nt = (((1,), (1,)), ((), ()))
    for h in range(N_HEADS):
        sl = slice(h * HEAD_DIM, (h + 1) * HEAD_DIM)
        q = q_ref[:, sl]
        s = []
        for c, k_ref in enumerate(k_refs):
            sc = lax.dot_general(q, k_ref[:, sl], nt, preferred_element_type=F32) + b_ref[h, :, c * qb:(c + 1) * qb]
            if penalties is not None and c < len(penalties):
                sc = sc + penalties[c]
            s.append(sc)
        m = jnp.max(functools.reduce(jnp.maximum, s), axis=-1, keepdims=True)
        p = [jnp.exp2(sc - m) for sc in s]
        l = jnp.sum(functools.reduce(jnp.add, p), axis=-1, keepdims=True)
        acc = None
        for pc, v_ref in zip(p, v_refs):
            pv = jnp.dot(pc.astype(BF16), v_ref[:, sl], preferred_element_type=F32)
            acc = pv if acc is None else acc + pv
        o_ref[:, sl] = (acc / l).astype(o_ref.dtype)


def _attn_kernel(q_ref, k0_ref, k1_ref, k2_ref, v0_ref, v1_ref, v2_ref, b_ref, o_ref, *, blocks_per_seq):
    i = pl.program_id(0) % blocks_per_seq
    k_refs = (k0_ref, k1_ref, k2_ref)
    v_refs = (v0_ref, v1_ref, v2_ref)
    first = i < ATT_KBLOCKS - 1

    @pl.when(first)
    def _():
        pen = [jnp.where(i >= ATT_KBLOCKS - 1 - c, np.float32(0.0), np.float32(NEG_INF))
               for c in range(ATT_KBLOCKS - 1)]
        _attn_heads(q_ref, k_refs, v_refs, b_ref, o_ref, pen)

    @pl.when(jnp.logical_not(first))
    def _():
        _attn_heads(q_ref, k_refs, v_refs, b_ref, o_ref, None)


def chunked_attention(cols, bias_tab, seq):
    n = cols.shape[0]
    qb = ATT_QBLOCK
    aw = N_HEADS * HEAD_DIM
    bps = seq // qb
    assert ATT_KBLOCKS == 3 and seq % qb == 0

    def kv_spec(col, back):
        return pl.BlockSpec((qb, aw), lambda g: (g - jnp.minimum(g % bps, back), col))

    return pl.pallas_call(
        functools.partial(_attn_kernel, blocks_per_seq=bps),
        grid=(n // qb,),
        in_specs=[pl.BlockSpec((qb, aw), lambda g: (g, 0)),
                  kv_spec(1, 2), kv_spec(1, 1), kv_spec(1, 0),
                  kv_spec(2, 2), kv_spec(2, 1), kv_spec(2, 0),
                  _resident(bias_tab.shape, lambda g: (0, 0, 0))],
        out_specs=pl.BlockSpec((qb, aw), lambda g: (g, 0)),
        out_shape=jax.ShapeDtypeStruct((n, aw), BF16),
        compiler_params=_cparams("arbitrary"),
        name="chunked_attention",
    )(cols, cols, cols, cols, cols, cols, cols, bias_tab)


def _rec_kernel(rx_ref, ry_ref, cw_ref, cb_ref, wg_ref, ba_ref, bx_ref, lam_ref, o_ref,
                xbuf, a_s, u_s, carry, *, rows, width, bw):
    t = pl.program_id(1)
    pad = SUBLANES

    @pl.when(t == 0)
    def _():
        xbuf[0:pad, :] = jnp.zeros((pad, width), F32)
        carry[...] = jnp.zeros((pad, width), F32)

    x = rx_ref[...].astype(F32)
    xbuf[pad:pad + rows, :] = x
    rc = cb_ref[...] + cw_ref[0:1, :] * xbuf[pad - 3:pad - 3 + rows, :]
    rc = rc + cw_ref[1:2, :] * xbuf[pad - 2:pad - 2 + rows, :]
    rc = rc + cw_ref[2:3, :] * xbuf[pad - 1:pad - 1 + rows, :]
    rc = rc + cw_ref[3:4, :] * x
    xbuf[0:pad, :] = xbuf[rows:rows + pad, :]
    u_s[...] = rc

    neg_c_sp = -LRU_C * _softplus(-lam_ref[...])
    for nb in range(width // bw):
        sl = slice(nb * bw, (nb + 1) * bw)
        rcn = u_s[:, sl]
        g = jnp.dot(rcn.astype(BF16), wg_ref[nb], preferred_element_type=F32)
        r = _sigmoid(g[:, :bw] + ba_ref[:, sl])
        ig = _sigmoid(g[:, bw:] + bx_ref[:, sl])
        log_a = neg_c_sp[:, sl] * r
        a_s[:, sl] = jnp.exp(log_a)
        th = jnp.tanh(log_a)
        one_m_a2 = -2.0 * th / (1.0 - th)
        root = one_m_a2 * lax.rsqrt(jnp.maximum(one_m_a2, np.float32(1e-30)))
        u_s[:, sl] = root * (ig * rcn)

    row = lax.broadcasted_iota(jnp.int32, (pad, width), 0)

    def group(gi, c):
        r0 = pl.multiple_of(gi * pad, pad)
        a = a_s[pl.ds(r0, pad), :]
        h = u_s[pl.ds(r0, pad), :]
        for s in (1, 2, 4):
            keep = row >= s
            a_prev = jnp.where(keep, pltpu.roll(a, s, 0), 1.0)
            h_prev = jnp.where(keep, pltpu.roll(h, s, 0), 0.0)
            h = a * h_prev + h
            a = a * a_prev
        h = h + a * c
        u_s[pl.ds(r0, pad), :] = h
        return jnp.broadcast_to(h[pad - 1:pad, :], (pad, width))

    carry[...] = lax.fori_loop(0, rows // pad, group, carry[...])
    o_ref[...] = (_gelu_tanh(ry_ref[...].astype(F32)) * u_s[...]).astype(o_ref.dtype)


def recurrent_branch(cols, conv_w, conv_b, w_gates, b_a, b_x, lam, batch, seq, rx_col, rows=256):
    n = cols.shape[0]
    width = conv_w.shape[1]
    bw = width // LRU_BLOCKS
    spb = seq // rows
    vec = lambda: pl.BlockSpec((1, width), lambda b, t: (0, 0))
    return pl.pallas_call(
        functools.partial(_rec_kernel, rows=rows, width=width, bw=bw),
        grid=(batch, spb),
        in_specs=[pl.BlockSpec((rows, width), lambda b, t: (b * spb + t, rx_col)),
                  pl.BlockSpec((rows, width), lambda b, t: (b * spb + t, rx_col + 1)),
                  pl.BlockSpec((CONV_W, width), lambda b, t: (0, 0)),
                  vec(),
                  pl.BlockSpec((LRU_BLOCKS, bw, 2 * bw), lambda b, t: (0, 0, 0)),
                  vec(), vec(), vec()],
        out_specs=pl.BlockSpec((rows, width), lambda b, t: (b * spb + t, 0)),
        out_shape=jax.ShapeDtypeStruct((n, width), BF16),
        scratch_shapes=[pltpu.VMEM((rows + SUBLANES, width), F32),
                        pltpu.VMEM((rows, width), F32),
                        pltpu.VMEM((rows, width), F32),
                        pltpu.VMEM((SUBLANES, width), F32)],
        compiler_params=_cparams("arbitrary", "arbitrary"),
        name="recurrent_branch",
    )(cols, cols, conv_w, conv_b.reshape(1, width), w_gates,
      b_a.reshape(1, width), b_x.reshape(1, width), lam.reshape(1, width))


def _proj_res_norm_kernel(x_ref, w_ref, h_ref, g_ref, hout_ref, n_ref, *extra):
    hn = h_ref[...] + jnp.dot(x_ref[...], w_ref[...], preferred_element_type=F32)
    hout_ref[...] = hn
    nrm = _rms(hn) * g_ref[...]
    n_ref[...] = nrm.astype(n_ref.dtype)
    for r in extra:
        r[...] = nrm.astype(r.dtype)


def proj_res_norm(x, w, h, g, norm_dtypes, tm=256):
    n, k = x.shape
    d = w.shape[1]
    row = lambda: pl.BlockSpec((tm, d), lambda i: (i, 0))
    return pl.pallas_call(
        _proj_res_norm_kernel,
        grid=(n // tm,),
        in_specs=[pl.BlockSpec((tm, k), lambda i: (i, 0)),
                  _resident((k, d), lambda i: (0, 0)),
                  row(),
                  pl.BlockSpec((1, d), lambda i: (0, 0))],
        out_specs=[row()] + [row() for _ in norm_dtypes],
        out_shape=[jax.ShapeDtypeStruct((n, d), F32)] + [jax.ShapeDtypeStruct((n, d), dt) for dt in norm_dtypes],
        compiler_params=_cparams("arbitrary"),
        name="proj_res_norm",
    )(x, w, h, g.reshape(1, d))


def _mix_out_kernel(*refs, n_gate_blocks):
    att_ref, rec_ref = refs[:2]
    ga_refs = refs[2:2 + n_gate_blocks]
    gb_refs = refs[2 + n_gate_blocks:2 + 2 * n_gate_blocks]
    wa_ref, wr_ref, wo_ref, h_ref, g_ref, hout_ref = refs[2 + 2 * n_gate_blocks:8 + 2 * n_gate_blocks]
    norm_refs = refs[8 + 2 * n_gate_blocks:]
    ya = jnp.dot(att_ref[...], wa_ref[...], preferred_element_type=F32)
    yr = jnp.dot(rec_ref[...], wr_ref[...], preferred_element_type=F32)
    ga = jnp.concatenate([r[...] for r in ga_refs], axis=1).astype(F32)
    gb = jnp.concatenate([r[...] for r in gb_refs], axis=1).astype(F32)
    merged = (_sigmoid(ga) * ya + _sigmoid(gb) * yr).astype(BF16)
    hn = h_ref[...] + jnp.dot(merged, wo_ref[...], preferred_element_type=F32)
    hout_ref[...] = hn
    nrm = _rms(hn) * g_ref[...]
    for r in norm_refs:
        r[...] = nrm.astype(r.dtype)


def mix_out(att, rec, cols, ga_col, w_att_o, w_rec_o, w_out, h, g, norm_dtypes, tm=256):
    n, aw = att.shape
    rw = rec.shape[1]
    d = w_out.shape[1]
    ngb = d // aw
    row = lambda: pl.BlockSpec((tm, d), lambda i: (i, 0))
    gate = lambda c: pl.BlockSpec((tm, aw), lambda i: (i, c))
    return pl.pallas_call(
        functools.partial(_mix_out_kernel, n_gate_blocks=ngb),
        grid=(n // tm,),
        in_specs=[pl.BlockSpec((tm, aw), lambda i: (i, 0)),
                  pl.BlockSpec((tm, rw), lambda i: (i, 0))]
                 + [gate(ga_col + c) for c in range(ngb)]
                 + [gate(ga_col + ngb + c) for c in range(ngb)]
                 + [_resident((aw, d), lambda i: (0, 0)),
                    _resident((rw, d), lambda i: (0, 0)),
                    _resident((d, d), lambda i: (0, 0)),
                    row(),
                    pl.BlockSpec((1, d), lambda i: (0, 0))],
        out_specs=[row()] + [row() for _ in norm_dtypes],
        out_shape=[jax.ShapeDtypeStruct((n, d), F32)] + [jax.ShapeDtypeStruct((n, d), dt) for dt in norm_dtypes],
        compiler_params=_cparams("arbitrary"),
        name="mix_out",
    )(att, rec, *([cols] * (2 * ngb)), w_att_o, w_rec_o, w_out, h, g.reshape(1, d))


def _ffn_up_kernel(x_ref, w1_ref, w3_ref, o_ref, w1_bf, w3_bf):
    @pl.when(pl.program_id(1) == 0)
    def _():
        w1_bf[...] = w1_ref[0].astype(BF16)
        w3_bf[...] = w3_ref[0].astype(BF16)

    x = x_ref[...]
    a = jnp.dot(x, w1_bf[...], preferred_element_type=F32)
    b = jnp.dot(x, w3_bf[...], preferred_element_type=F32)
    o_ref[...] = ((a * _sigmoid(a)) * b).astype(o_ref.dtype)


def ffn_up(xn, w1, w3, layer, tm=1024, tn=512):
    n, d = xn.shape
    f = w1.shape[2]
    w_spec = lambda: pl.BlockSpec((1, d, tn), lambda j, i: (layer, 0, j))
    return pl.pallas_call(
        _ffn_up_kernel,
        grid=(f // tn, n // tm),
        in_specs=[pl.BlockSpec((tm, d), lambda j, i: (i, 0)), w_spec(), w_spec()],
        out_specs=pl.BlockSpec((tm, tn), lambda j, i: (i, j)),
        out_shape=jax.ShapeDtypeStruct((n, f), BF16),
        scratch_shapes=[pltpu.VMEM((d, tn), BF16), pltpu.VMEM((d, tn), BF16)],
        compiler_params=_cparams("arbitrary", "arbitrary"),
        name="ffn_up",
    )(xn, w1, w3)


ROUTE_E0, ROUTE_E1, ROUTE_G0, ROUTE_G1, ROUTE_R0, ROUTE_R1 = range(6)


def _router_kernel(x_ref, w_ref, tri_ref, route_ref, cnt_ref, carry):
    @pl.when(pl.program_id(0) == 0)
    def _():
        carry[...] = jnp.zeros_like(carry)

    tm = x_ref.shape[0]
    logits = jnp.dot(x_ref[...], w_ref[...], preferred_element_type=F32)
    lane = lax.broadcasted_iota(jnp.int32, (tm, LANES), 1).astype(F32)
    neg = np.float32(-np.inf)
    lg = jnp.where(lane < N_EXPERTS, logits, neg)
    m0 = jnp.max(lg, axis=-1, keepdims=True)
    e0 = jnp.min(jnp.where(lg == m0, lane, np.float32(LANES)), axis=-1, keepdims=True)
    lg1 = jnp.where(lane == e0, neg, lg)
    m1 = jnp.max(lg1, axis=-1, keepdims=True)
    e1 = jnp.min(jnp.where(lg1 == m1, lane, np.float32(LANES)), axis=-1, keepdims=True)
    ex = jnp.exp(m1 - m0)
    g0 = 1.0 / (1.0 + ex)
    g1 = ex / (1.0 + ex)
    sel0 = lane == e0
    sel1 = lane == e1
    member = jnp.where(sel0 | sel1, np.float32(1.0), np.float32(0.0))
    incl = jnp.dot(tri_ref[...], member.astype(BF16), preferred_element_type=F32)
    before = carry[...] + incl - member
    r0 = jnp.sum(jnp.where(sel0, before, 0.0), axis=-1, keepdims=True)
    r1 = jnp.sum(jnp.where(sel1, before, 0.0), axis=-1, keepdims=True)
    total = carry[...] + incl[tm - 1:tm, :]
    carry[...] = total
    cnt_ref[...] = jnp.broadcast_to(total, cnt_ref.shape)
    out = jnp.zeros((tm, LANES), F32)
    for k, v in ((ROUTE_E0, e0), (ROUTE_E1, e1), (ROUTE_G0, g0), (ROUTE_G1, g1), (ROUTE_R0, r0), (ROUTE_R1, r1)):
        out = jnp.where(lane == k, v, out)
    route_ref[...] = out


def router(hn, w_router, tm=512):
    n, d = hn.shape
    wr = jnp.zeros((d, LANES), BF16).at[:, :N_EXPERTS].set(w_router.astype(BF16))
    tri = jnp.asarray(np.tril(np.ones((tm, tm), np.float32)), BF16)
    return pl.pallas_call(
        _router_kernel,
        grid=(n // tm,),
        in_specs=[pl.BlockSpec((tm, d), lambda i: (i, 0)),
                  pl.BlockSpec((d, LANES), lambda i: (0, 0)),
                  pl.BlockSpec((tm, tm), lambda i: (0, 0))],
        out_specs=[pl.BlockSpec((tm, LANES), lambda i: (i, 0)),
                   pl.BlockSpec((SUBLANES, LANES), lambda i: (0, 0))],
        out_shape=[jax.ShapeDtypeStruct((n, LANES), F32),
                   jax.ShapeDtypeStruct((SUBLANES, LANES), F32)],
        scratch_shapes=[pltpu.VMEM((1, LANES), F32)],
        compiler_params=_cparams("arbitrary"),
        name="router",
    )(hn, wr, tri)


def _dispatch_kernel(dest_ref, zero_ref, h_ref, g_ref, xs_ref, zbuf, nbuf, sems, zsem, *, tm):
    step = pl.program_id(0)
    n_steps = pl.num_programs(0)
    slot = lax.rem(step, 2)
    eb = EXPERT_BLOCK

    @pl.when(step == 0)
    def _():
        zbuf[...] = jnp.zeros_like(zbuf)
        for z in range(zero_ref.shape[0]):
            @pl.when(zero_ref[z] >= 0)
            def _():
                cp = pltpu.make_async_copy(zbuf, xs_ref.at[pl.ds(pl.multiple_of(zero_ref[z], eb), eb)], zsem)
                cp.start()
                cp.wait()

    base = step * (tm * TOP_K)
    nbuf[slot] = _rms(h_ref[...]) * g_ref[...]

    def row_copy(s, r, d):
        return pltpu.make_async_copy(nbuf.at[s, pl.ds(r, 1)], xs_ref.at[pl.ds(d, 1)], sems.at[s])

    def issue(r, carry):
        for k in range(TOP_K):
            row_copy(slot, r, dest_ref[base + TOP_K * r + k]).start()
        return carry

    def wait_all(s):
        for _ in range(TOP_K * tm):
            row_copy(s, 0, 0).wait()

    lax.fori_loop(0, tm, issue, 0, unroll=ROW_DMA_UNROLL)

    @pl.when(step >= 1)
    def _():
        wait_all(1 - slot)

    @pl.when(step == n_steps - 1)
    def _():
        wait_all(slot)


def dispatch(h, g, dest, zero_start, n_rows, tm=256):
    n, d = h.shape
    return pl.pallas_call(
        functools.partial(_dispatch_kernel, tm=tm),
        grid_spec=pltpu.PrefetchScalarGridSpec(
            num_scalar_prefetch=2,
            grid=(n // tm,),
            in_specs=[pl.BlockSpec((tm, d), lambda i, dest, tail: (i, 0)),
                      pl.BlockSpec((1, d), lambda i, dest, tail: (0, 0))],
            out_specs=pl.BlockSpec(memory_space=pl.ANY),
            scratch_shapes=[pltpu.VMEM((EXPERT_BLOCK, d), F32),
                            pltpu.VMEM((2, tm, d), F32),
                            pltpu.SemaphoreType.DMA((2,)),
                            pltpu.SemaphoreType.DMA]),
        out_shape=jax.ShapeDtypeStruct((n_rows, d), F32),
        compiler_params=_cparams("arbitrary"),
        name="moe_dispatch",
    )(dest, zero_start, h, g.reshape(1, d))


def _grouped_rows_kernel(start_ref, count_ref, spare_ref, x_hbm, *refs, n_weights, tm, tn, tile_fn):
    w_refs = refs[:n_weights]
    o_hbm = refs[n_weights]
    w_bf = refs[n_weights + 1:2 * n_weights + 1]
    xbuf, obuf, zbuf, xsem, osem, zsem = refs[2 * n_weights + 1:]
    j = pl.program_id(0)
    e = pl.program_id(1)
    nb = count_ref[e]
    first = start_ref[e]
    col0 = pl.multiple_of(j * tn, tn)

    n_in = xbuf.shape[0]
    n_out = obuf.shape[0]
    ub = xbuf.shape[1] // tm
    n_units = lax.div(nb + (ub - 1), ub)
    n_piece = -(-(tm * xbuf.shape[2] * xbuf.dtype.itemsize) // MAX_PLAIN_DMA_BYTES)
    piece = tm // n_piece

    def x_copies(u, b):
        slot = lax.rem(u, n_in)
        r0 = (first + u * ub + b) * tm
        return [pltpu.make_async_copy(x_hbm.at[pl.ds(pl.multiple_of(r0 + p * piece, piece), piece)],
                                      xbuf.at[slot, pl.ds(b * tm + p * piece, piece)], xsem.at[slot])
                for p in range(n_piece)]

    def o_copy(u, b):
        slot = lax.rem(u, n_out)
        rows = pl.ds(pl.multiple_of((first + u * ub + b) * tm, tm), tm)
        return pltpu.make_async_copy(obuf.at[slot, pl.ds(b * tm, tm)], o_hbm.at[rows, pl.ds(col0, tn)],
                                     osem.at[slot])

    def for_blocks(u, fn):
        fn(0)
        for b in range(1, ub):
            @pl.when(u * ub + b < nb)
            def _():
                fn(b)

    def x_start(u):
        for_blocks(u, lambda b: [cp.start(priority=1) for cp in x_copies(u, b)])

    def x_wait(u):
        for_blocks(u, lambda b: [cp.wait() for cp in x_copies(u, b)])

    @pl.when(nb > 0)
    def _():
        for ahead in range(n_in - 1):
            @pl.when(ahead < n_units)
            def _():
                x_start(ahead)
        for wb, w in zip(w_bf, w_refs):
            wb[...] = w[0, 0].astype(BF16)

        def unit(u, carry):
            x_wait(u)

            @pl.when(u + (n_in - 1) < n_units)
            def _():
                x_start(u + (n_in - 1))

            @pl.when(u >= n_out)
            def _():
                for_blocks(u - n_out, lambda b: o_copy(u - n_out, b).wait())

            slot_in = lax.rem(u, n_in)
            slot_out = lax.rem(u, n_out)
            have = nb - u * ub

            @pl.when(have >= ub)
            def _():
                obuf[slot_out] = tile_fn(xbuf[slot_in], *w_bf)

            for short in range(1, ub):
                @pl.when(have == short)
                def _():
                    obuf[slot_out, :short * tm] = tile_fn(xbuf[slot_in, :short * tm], *w_bf)

            for_blocks(u, lambda b: o_copy(u, b).start())
            return carry

        lax.fori_loop(0, n_units, unit, 0)

        for back in range(n_out, 0, -1):
            @pl.when(n_units >= back)
            def _():
                for_blocks(n_units - back, lambda b: o_copy(n_units - back, b).wait())

    @pl.when(e == pl.num_programs(1) - 1)
    def _():
        zbuf[...] = jnp.zeros_like(zbuf)

        def z_copy(z):
            rows = pl.ds(pl.multiple_of(spare_ref[z], tm), tm)
            return pltpu.make_async_copy(zbuf, o_hbm.at[rows, pl.ds(col0, tn)], zsem)

        for z in range(spare_ref.shape[0]):
            @pl.when(spare_ref[z] >= 0)
            def _():
                z_copy(z).start()
        for z in range(spare_ref.shape[0]):
            @pl.when(spare_ref[z] >= 0)
            def _():
                z_copy(z).wait()


def _grouped_rows(x, weights, layer, groups, out_dtype, tn, tile_fn, name):
    start_blk, count_blk, spare_start = groups
    rows, k = x.shape
    n_exp = weights[0].shape[1]
    width = weights[0].shape[3]
    tm = EXPERT_BLOCK
    w_spec = lambda: pl.BlockSpec((1, 1, k, tn), lambda j, e, *_: (layer, e, 0, j))
    return pl.pallas_call(
        functools.partial(_grouped_rows_kernel, n_weights=len(weights), tm=tm, tn=tn, tile_fn=tile_fn),
        grid_spec=pltpu.PrefetchScalarGridSpec(
            num_scalar_prefetch=3,
            grid=(width // tn, n_exp),
            in_specs=[pl.BlockSpec(memory_space=pl.ANY)] + [w_spec() for _ in weights],
            out_specs=pl.BlockSpec(memory_space=pl.ANY),
            scratch_shapes=[pltpu.VMEM((k, tn), BF16) for _ in weights] + [
                pltpu.VMEM((GROUPED_IN_SLOTS, GROUPED_UNIT_BLOCKS * tm, k), x.dtype),
                pltpu.VMEM((GROUPED_OUT_SLOTS, GROUPED_UNIT_BLOCKS * tm, tn), out_dtype),
                pltpu.VMEM((tm, tn), out_dtype),
                pltpu.SemaphoreType.DMA((GROUPED_IN_SLOTS,)),
                pltpu.SemaphoreType.DMA((GROUPED_OUT_SLOTS,)),
                pltpu.SemaphoreType.DMA]),
        out_shape=jax.ShapeDtypeStruct((rows, width), out_dtype),
        compiler_params=_cparams("arbitrary", "arbitrary"),
        name=name,
    )(start_blk, count_blk, spare_start, x, *weights)


def _swiglu_tile(x, w1_bf, w3_bf):
    xb = x.astype(BF16)
    a = jnp.dot(xb, w1_bf[...], preferred_element_type=F32)
    b = jnp.dot(xb, w3_bf[...], preferred_element_type=F32)
    return ((a * _sigmoid(a)) * b).astype(BF16)


def _matmul_tile(x, w_bf):
    return jnp.dot(x, w_bf[...], preferred_element_type=F32)


def moe_up(xs, w1, w3, layer, groups, tn=1024):
    return _grouped_rows(xs, (w1, w3), layer, groups, BF16, tn, _swiglu_tile, "moe_up")


def moe_down(act, w2, layer, groups, tn=512):
    return _grouped_rows(act, (w2,), layer, groups, F32, tn, _matmul_tile, "moe_down")


def _combine_kernel(dest_ref, ys_ref, route_ref, h_ref, g_ref, hout_ref, n_ref, bufs, sems, *, tm):
    step = pl.program_id(0)
    n_steps = pl.num_programs(0)
    slot = lax.rem(step, 2)

    def row_copy(k, s, r, d):
        return pltpu.make_async_copy(ys_ref.at[pl.ds(d, 1)], bufs.at[k, s, pl.ds(r, 1)], sems.at[k, s])

    def gather(st, s):
        base = st * (tm * TOP_K)

        def issue(r, carry):
            for k in range(TOP_K):
                row_copy(k, s, r, dest_ref[base + TOP_K * r + k]).start()
            return carry

        lax.fori_loop(0, tm, issue, 0, unroll=ROW_DMA_UNROLL)

    @pl.when(step == 0)
    def _():
        gather(0, 0)

    @pl.when(step + 1 < n_steps)
    def _():
        gather(step + 1, 1 - slot)

    for k in range(TOP_K):
        for _ in range(tm):
            row_copy(k, slot, 0, 0).wait()

    route = route_ref[...]
    g0 = route[:, ROUTE_G0:ROUTE_G0 + 1]
    g1 = route[:, ROUTE_G1:ROUTE_G1 + 1]
    hn = h_ref[...] + (g0 * bufs[0, slot] + g1 * bufs[1, slot])
    hout_ref[...] = hn
    n_ref[...] = (_rms(hn) * g_ref[...]).astype(n_ref.dtype)


def combine(ys, dest, route, h, g, norm_dtype, tm=256):
    n, d = h.shape
    row = lambda: pl.BlockSpec((tm, d), lambda i, dest: (i, 0))
    return pl.pallas_call(
        functools.partial(_combine_kernel, tm=tm),
        grid_spec=pltpu.PrefetchScalarGridSpec(
            num_scalar_prefetch=1,
            grid=(n // tm,),
            in_specs=[pl.BlockSpec(memory_space=pl.ANY),
                      pl.BlockSpec((tm, LANES), lambda i, dest: (i, 0)),
                      row(),
                      pl.BlockSpec((1, d), lambda i, dest: (0, 0))],
            out_specs=[row(), row()],
            scratch_shapes=[pltpu.VMEM((TOP_K, 2, tm, d), F32),
                            pltpu.SemaphoreType.DMA((TOP_K, 2))]),
        out_shape=[jax.ShapeDtypeStruct((n, d), F32), jax.ShapeDtypeStruct((n, d), norm_dtype)],
        compiler_params=_cparams("arbitrary"),
        name="moe_combine",
    )(dest, ys, route, h, g.reshape(1, d))


def moe_layer(hn_bf, h, g_ffn, w_router, w1, w3, w2, layer, g_next, norm_dtype):
    n = h.shape[0]
    eb = EXPERT_BLOCK
    route, cnt = router(hn_bf, w_router)
    counts = cnt[0, :N_EXPERTS].astype(jnp.int32)
    padded = (counts + eb - 1) // eb * eb
    pend = jnp.cumsum(padded)
    pstart = pend - padded
    experts = route[:, ROUTE_E0:ROUTE_E1 + 1].astype(jnp.int32)
    ranks = route[:, ROUTE_R0:ROUTE_R1 + 1].astype(jnp.int32)
    dest = (pstart[experts] + ranks).reshape(-1)
    n_blocks = -(-(n * TOP_K) // eb) + N_EXPERTS
    tail_start = jnp.where(padded > 0, pend - eb, -1)
    spare = pend[-1] + jnp.arange(N_EXPERTS) * eb
    spare_start = jnp.where(spare < n_blocks * eb, spare, -1).astype(jnp.int32)
    zero_start = jnp.concatenate([tail_start.astype(jnp.int32), spare_start])
    groups = ((pstart // eb).astype(jnp.int32), (padded // eb).astype(jnp.int32), spare_start)
    xs = dispatch(h, g_ffn, dest, zero_start, n_blocks * eb)
    act = moe_up(xs, w1, w3, layer, groups)
    ys = moe_down(act, w2, layer, groups)
    return combine(ys, dest, route, h, g_next, norm_dtype)


def kernel(x, norm_mix_g, norm_ffn_g, norm_final_g, w_in, rel_bias, conv_w, conv_b, lru_wa, lru_ba, lru_wx, lru_bx, lru_lambda, w_att_o, w_rec_o, w_out, ffn_w1, ffn_w3, ffn_w2, router_w, moe_w1, moe_w3, moe_w2):
    b, s, d = x.shape
    n = b * s
    depth = w_in.shape[0]
    aw = N_HEADS * HEAD_DIM
    lw = conv_w.shape[2]
    assert lw == aw and d % aw == 0, "column-block indexing assumes equal branch widths"
    h = x.reshape(n, d)
    xn = rmsnorm(h, norm_mix_g[0], BF16)
    out = None
    for l in range(depth):
        cols = in_projection(xn, w_in, l, aw, HEAD_DIM ** -0.5 * LOG2_E)
        att = chunked_attention(cols, _rel_bias_table(rel_bias[l]), s)
        w_gates = jnp.concatenate([lru_wa[l], lru_wx[l]], axis=-1).astype(BF16)
        rec = recurrent_branch(cols, conv_w[l], conv_b[l], w_gates, lru_ba[l], lru_bx[l], lru_lambda[l],
                               b, s, rx_col=3)
        last = l == depth - 1
        g_next = norm_final_g if last else norm_mix_g[l + 1]
        next_dtype = F32 if last else BF16
        i = l // 2
        dense = l % 2 == 0
        h, hn = mix_out(att, rec, cols, 5, w_att_o[l].astype(BF16), w_rec_o[l].astype(BF16),
                        w_out[l].astype(BF16), h, norm_ffn_g[l], [BF16])
        if dense:
            act = ffn_up(hn, ffn_w1, ffn_w3, i)
            h, xn = proj_res_norm(act, ffn_w2[i].astype(BF16), h, g_next, [next_dtype])
        else:
            h, xn = moe_layer(hn, h, norm_ffn_g[l], router_w[i], moe_w1, moe_w3, moe_w2, i, g_next, next_dtype)
        out = xn
    return out.reshape(b, s, d)
```

```python
import functools

import numpy as np
import jax
import jax.numpy as jnp
from jax import lax
from jax.experimental import pallas as pl
from jax.experimental.pallas import tpu as pltpu

CHUNK = 64
LEFT_CHUNKS = 8
N_HEADS = 8
HEAD_DIM = 128
MAX_REL_DIST = 256
LRU_BLOCKS = 8
CONV_W = 4
LRU_C = 8.0
N_EXPERTS = 8
TOP_K = 2
EXPERT_BLOCK = 256
RMS_EPS = 1e-6
NEG_INF = -1e30
LOG2_E = float(np.log2(np.e))

LANES = 128
SUBLANES = 8
VMEM_LIMIT_BYTES = 56 * 1024 * 1024

ATT_QBLOCK = 4 * CHUNK
ATT_KBLOCKS = (LEFT_CHUNKS * CHUNK) // ATT_QBLOCK + 1

ROW_DMA_UNROLL = 8

GROUPED_UNIT_BLOCKS = 2
GROUPED_IN_SLOTS = 2
GROUPED_OUT_SLOTS = 2
MAX_PLAIN_DMA_BYTES = 2 * 1024 * 1024

F32 = jnp.float32
BF16 = jnp.bfloat16


def _cparams(*sem):
    return pltpu.CompilerParams(dimension_semantics=sem, vmem_limit_bytes=VMEM_LIMIT_BYTES)


def _resident(shape, index_map):
    return pl.BlockSpec(shape, index_map, pipeline_mode=pl.Buffered(1))


def _rms(x):
    return x * lax.rsqrt(jnp.mean(x * x, axis=-1, keepdims=True) + RMS_EPS)


def _sigmoid(x):
    return 1.0 / (1.0 + jnp.exp(-x))


def _gelu_tanh(x):
    c = np.float32(np.sqrt(2.0 / np.pi))
    return 0.5 * x * (1.0 + jnp.tanh(c * (x + 0.044715 * (x * x * x))))


def _softplus(x):
    return jnp.maximum(x, 0.0) + jnp.log1p(jnp.exp(-jnp.abs(x)))


def _rmsnorm_kernel(x_ref, g_ref, o_ref):
    o_ref[...] = (_rms(x_ref[...]) * g_ref[...]).astype(o_ref.dtype)


def rmsnorm(x, g, out_dtype, tm=512):
    n, d = x.shape
    return pl.pallas_call(
        _rmsnorm_kernel,
        grid=(n // tm,),
        in_specs=[pl.BlockSpec((tm, d), lambda i: (i, 0)),
                  pl.BlockSpec((1, d), lambda i: (0, 0))],
        out_specs=pl.BlockSpec((tm, d), lambda i: (i, 0)),
        out_shape=jax.ShapeDtypeStruct((n, d), out_dtype),
        compiler_params=_cparams("arbitrary"),
        name="rmsnorm",
    )(x, g.reshape(1, d))


def _inproj_kernel(x_ref, w_ref, o_ref, w_bf, *, q_scale):
    @pl.when(pl.program_id(1) == 0)
    def _():
        w_bf[...] = w_ref[0].astype(BF16)

    acc = jnp.dot(x_ref[...], w_bf[...], preferred_element_type=F32)
    s = jnp.where(pl.program_id(0) == 0, np.float32(q_scale), np.float32(1.0))
    o_ref[...] = (acc * s).astype(o_ref.dtype)


def in_projection(xn, w, layer, q_width, q_scale, tm=1024):
    n, d = xn.shape
    width = w.shape[2]
    tn = q_width
    return pl.pallas_call(
        functools.partial(_inproj_kernel, q_scale=q_scale),
        grid=(width // tn, n // tm),
        in_specs=[pl.BlockSpec((tm, d), lambda j, i: (i, 0)),
                  pl.BlockSpec((1, d, tn), lambda j, i: (layer, 0, j))],
        out_specs=pl.BlockSpec((tm, tn), lambda j, i: (i, j)),
        out_shape=jax.ShapeDtypeStruct((n, width), BF16),
        scratch_shapes=[pltpu.VMEM((d, tn), BF16)],
        compiler_params=_cparams("arbitrary", "arbitrary"),
        name="in_projection",
    )(xn, w)


def _rel_bias_table(rel_bias):
    qb = ATT_QBLOCK
    kw = ATT_KBLOCKS * ATT_QBLOCK
    n_heads, n_rel = rel_bias.shape
    period = qb + kw
    dist = LEFT_CHUNKS * CHUNK + (qb - 1) - np.arange(period)
    idx = np.clip(dist, -(CHUNK - 1), MAX_REL_DIST) + CHUNK - 1
    n_far = int(np.sum(idx == n_rel - 1)) - 1
    n_near = int(np.sum(idx == 0)) - 1
    assert np.array_equal(idx, np.concatenate([np.full(n_far, n_rel - 1), np.arange(n_rel)[::-1], np.full(n_near, 0)]))
    rb = rel_bias.astype(F32)
    diag = jnp.concatenate([jnp.broadcast_to(rb[:, -1:], (n_heads, n_far)), rb[:, ::-1],
                            jnp.broadcast_to(rb[:, :1], (n_heads, n_near))], axis=1)
    rep = jnp.broadcast_to(diag[:, None, :], (n_heads, qb, period)).reshape(n_heads, qb * period)
    skew = rep[:, :qb * (period - 1)].reshape(n_heads, qb, period - 1)
    tab = skew[:, :, qb - 1:qb - 1 + kw]
    qc = np.arange(qb)[:, None] // CHUNK
    kc = np.arange(kw)[None, :] // CHUNK
    valid = (kc >= qc) & (kc <= qc + LEFT_CHUNKS)
    return jnp.where(valid[None], tab * LOG2_E, NEG_INF)


def _attn_heads(q_ref, k_refs, v_refs, b_ref, o_ref, penalties):
    qb = ATT_QBLOCK
    nt = (((1,), (1,)), ((), ()))
    for h in range(N_HEADS):
        sl = slice(h * HEAD_DIM, (h + 1) * HEAD_DIM)
        q = q_ref[:, sl]
        s = []
        for c, k_ref in enumerate(k_refs):
            sc = lax.dot_general(q, k_ref[:, sl], nt, preferred_element_type=F32) + b_ref[h, :, c * qb:(c + 1) * qb]
            if penalties is not None and c < len(penalties):
                sc = sc + penalties[c]
            s.append(sc)
        m = jnp.max(functools.reduce(jnp.maximum, s), axis=-1, keepdims=True)
        p = [jnp.exp2(sc - m) for sc in s]
        l = jnp.sum(functools.reduce(jnp.add, p), axis=-1, keepdims=True)
        acc = None
        for pc, v_ref in zip(p, v_refs):
            pv = jnp.dot(pc.astype(BF16), v_ref[:, sl], preferred_element_type=F32)
            acc = pv if acc is None else acc + pv
        o_ref[:, sl] = (acc / l).astype(o_ref.dtype)


def _attn_kernel(q_ref, k0_ref, k1_ref, k2_ref, v0_ref, v1_ref, v2_ref, b_ref, o_ref, *, blocks_per_seq):
    i = pl.program_id(0) % blocks_per_seq
    k_refs = (k0_ref, k1_ref, k2_ref)
    v_refs = (v0_ref, v1_ref, v2_ref)
    first = i < ATT_KBLOCKS - 1

    @pl.when(first)
    def _():
        pen = [jnp.where(i >= ATT_KBLOCKS - 1 - c, np.float32(0.0), np.float32(NEG_INF))
               for c in range(ATT_KBLOCKS - 1)]
        _attn_heads(q_ref, k_refs, v_refs, b_ref, o_ref, pen)

    @pl.when(jnp.logical_not(first))
    def _():
        _attn_heads(q_ref, k_refs, v_refs, b_ref, o_ref, None)


def chunked_attention(cols, bias_tab, seq):
    n = cols.shape[0]
    qb = ATT_QBLOCK
    aw = N_HEADS * HEAD_DIM
    bps = seq // qb
    assert ATT_KBLOCKS == 3 and seq % qb == 0

    def kv_spec(col, back):
        return pl.BlockSpec((qb, aw), lambda g: (g - jnp.minimum(g % bps, back), col))

    return pl.pallas_call(
        functools.partial(_attn_kernel, blocks_per_seq=bps),
        grid=(n // qb,),
        in_specs=[pl.BlockSpec((qb, aw), lambda g: (g, 0)),
                  kv_spec(1, 2), kv_spec(1, 1), kv_spec(1, 0),
                  kv_spec(2, 2), kv_spec(2, 1), kv_spec(2, 0),
                  _resident(bias_tab.shape, lambda g: (0, 0, 0))],
        out_specs=pl.BlockSpec((qb, aw), lambda g: (g, 0)),
        out_shape=jax.ShapeDtypeStruct((n, aw), BF16),
        compiler_params=_cparams("arbitrary"),
        name="chunked_attention",
    )(cols, cols, cols, cols, cols, cols, cols, bias_tab)


def _rec_kernel(rx_ref, ry_ref, cw_ref, cb_ref, wg_ref, ba_ref, bx_ref, lam_ref, o_ref,
                xbuf, a_s, u_s, carry, *, rows, width, bw):
    t = pl.program_id(1)
    pad = SUBLANES

    @pl.when(t == 0)
    def _():
        xbuf[0:pad, :] = jnp.zeros((pad, width), F32)
        carry[...] = jnp.zeros((pad, width), F32)

    x = rx_ref[...].astype(F32)
    xbuf[pad:pad + rows, :] = x
    rc = cb_ref[...] + cw_ref[0:1, :] * xbuf[pad - 3:pad - 3 + rows, :]
    rc = rc + cw_ref[1:2, :] * xbuf[pad - 2:pad - 2 + rows, :]
    rc = rc + cw_ref[2:3, :] * xbuf[pad - 1:pad - 1 + rows, :]
    rc = rc + cw_ref[3:4, :] * x
    xbuf[0:pad, :] = xbuf[rows:rows + pad, :]
    u_s[...] = rc

    neg_c_sp = -LRU_C * _softplus(-lam_ref[...])
    for nb in range(width // bw):
        sl = slice(nb * bw, (nb + 1) * bw)
        rcn = u_s[:, sl]
        g = jnp.dot(rcn.astype(BF16), wg_ref[nb], preferred_element_type=F32)
        r = _sigmoid(g[:, :bw] + ba_ref[:, sl])
        ig = _sigmoid(g[:, bw:] + bx_ref[:, sl])
        log_a = neg_c_sp[:, sl] * r
        a_s[:, sl] = jnp.exp(log_a)
        th = jnp.tanh(log_a)
        one_m_a2 = -2.0 * th / (1.0 - th)
        root = one_m_a2 * lax.rsqrt(jnp.maximum(one_m_a2, np.float32(1e-30)))
        u_s[:, sl] = root * (ig * rcn)

    row = lax.broadcasted_iota(jnp.int32, (pad, width), 0)

    def group(gi, c):
        r0 = pl.multiple_of(gi * pad, pad)
        a = a_s[pl.ds(r0, pad), :]
        h = u_s[pl.ds(r0, pad), :]
        for s in (1, 2, 4):
            keep = row >= s
            a_prev = jnp.where(keep, pltpu.roll(a, s, 0), 1.0)
            h_prev = jnp.where(keep, pltpu.roll(h, s, 0), 0.0)
            h = a * h_prev + h
            a = a * a_prev
        h = h + a * c
        u_s[pl.ds(r0, pad), :] = h
        return jnp.broadcast_to(h[pad - 1:pad, :], (pad, width))

    carry[...] = lax.fori_loop(0, rows // pad, group, carry[...])
    o_ref[...] = (_gelu_tanh(ry_ref[...].astype(F32)) * u_s[...]).astype(o_ref.dtype)


def recurrent_branch(cols, conv_w, conv_b, w_gates, b_a, b_x, lam, batch, seq, rx_col, rows=256):
    n = cols.shape[0]
    width = conv_w.shape[1]
    bw = width // LRU_BLOCKS
    spb = seq // rows
    vec = lambda: pl.BlockSpec((1, width), lambda b, t: (0, 0))
    return pl.pallas_call(
        functools.partial(_rec_kernel, rows=rows, width=width, bw=bw),
        grid=(batch, spb),
        in_specs=[pl.BlockSpec((rows, width), lambda b, t: (b * spb + t, rx_col)),
                  pl.BlockSpec((rows, width), lambda b, t: (b * spb + t, rx_col + 1)),
                  pl.BlockSpec((CONV_W, width), lambda b, t: (0, 0)),
                  vec(),
                  pl.BlockSpec((LRU_BLOCKS, bw, 2 * bw), lambda b, t: (0, 0, 0)),
                  vec(), vec(), vec()],
        out_specs=pl.BlockSpec((rows, width), lambda b, t: (b * spb + t, 0)),
        out_shape=jax.ShapeDtypeStruct((n, width), BF16),
        scratch_shapes=[pltpu.VMEM((rows + SUBLANES, width), F32),
                        pltpu.VMEM((rows, width), F32),
                        pltpu.VMEM((rows, width), F32),
                        pltpu.VMEM((SUBLANES, width), F32)],
        compiler_params=_cparams("arbitrary", "arbitrary"),
        name="recurrent_branch",
    )(cols, cols, conv_w, conv_b.reshape(1, width), w_gates,
      b_a.reshape(1, width), b_x.reshape(1, width), lam.reshape(1, width))


def _proj_res_norm_kernel(x_ref, w_ref, h_ref, g_ref, hout_ref, n_ref, *extra):
    hn = h_ref[...] + jnp.dot(x_ref[...], w_ref[...], preferred_element_type=F32)
    hout_ref[...] = hn
    nrm = _rms(hn) * g_ref[...]
    n_ref[...] = nrm.astype(n_ref.dtype)
    for r in extra:
        r[...] = nrm.astype(r.dtype)


def proj_res_norm(x, w, h, g, norm_dtypes, tm=256):
    n, k = x.shape
    d = w.shape[1]
    row = lambda: pl.BlockSpec((tm, d), lambda i: (i, 0))
    return pl.pallas_call(
        _proj_res_norm_kernel,
        grid=(n // tm,),
        in_specs=[pl.BlockSpec((tm, k), lambda i: (i, 0)),
                  _resident((k, d), lambda i: (0, 0)),
                  row(),
                  pl.BlockSpec((1, d), lambda i: (0, 0))],
        out_specs=[row()] + [row() for _ in norm_dtypes],
        out_shape=[jax.ShapeDtypeStruct((n, d), F32)] + [jax.ShapeDtypeStruct((n, d), dt) for dt in norm_dtypes],
        compiler_params=_cparams("arbitrary"),
        name="proj_res_norm",
    )(x, w, h, g.reshape(1, d))


def _mix_out_kernel(*refs, n_gate_blocks):
    att_ref, rec_ref = refs[:2]
    ga_refs = refs[2:2 + n_gate_blocks]
    gb_refs = refs[2 + n_gate_blocks:2 + 2 * n_gate_blocks]
    wa_ref, wr_ref, wo_ref, h_ref, g_ref, hout_ref = refs[2 + 2 * n_gate_blocks:8 + 2 * n_gate_blocks]
    norm_refs = refs[8 + 2 * n_gate_blocks:]
    ya = jnp.dot(att_ref[...], wa_ref[...], preferred_element_type=F32)
    yr = jnp.dot(rec_ref[...], wr_ref[...], preferred_element_type=F32)
    ga = jnp.concatenate([r[...] for r in ga_refs], axis=1).astype(F32)
    gb = jnp.concatenate([r[...] for r in gb_refs], axis=1).astype(F32)
    merged = (_sigmoid(ga) * ya + _sigmoid(gb) * yr).astype(BF16)
    hn = h_ref[...] + jnp.dot(merged, wo_ref[...], preferred_element_type=F32)
    hout_ref[...] = hn
    nrm = _rms(hn) * g_ref[...]
    for r in norm_refs:
        r[...] = nrm.astype(r.dtype)


def mix_out(att, rec, cols, ga_col, w_att_o, w_rec_o, w_out, h, g, norm_dtypes, tm=256):
    n, aw = att.shape
    rw = rec.shape[1]
    d = w_out.shape[1]
    ngb = d // aw
    row = lambda: pl.BlockSpec((tm, d), lambda i: (i, 0))
    gate = lambda c: pl.BlockSpec((tm, aw), lambda i: (i, c))
    return pl.pallas_call(
        functools.partial(_mix_out_kernel, n_gate_blocks=ngb),
        grid=(n // tm,),
        in_specs=[pl.BlockSpec((tm, aw), lambda i: (i, 0)),
                  pl.BlockSpec((tm, rw), lambda i: (i, 0))]
                 + [gate(ga_col + c) for c in range(ngb)]
                 + [gate(ga_col + ngb + c) for c in range(ngb)]
                 + [_resident((aw, d), lambda i: (0, 0)),
                    _resident((rw, d), lambda i: (0, 0)),
                    _resident((d, d), lambda i: (0, 0)),
                    row(),
                    pl.BlockSpec((1, d), lambda i: (0, 0))],
        out_specs=[row()] + [row() for _ in norm_dtypes],
        out_shape=[jax.ShapeDtypeStruct((n, d), F32)] + [jax.ShapeDtypeStruct((n, d), dt) for dt in norm_dtypes],
        compiler_params=_cparams("arbitrary"),
        name="mix_out",
    )(att, rec, *([cols] * (2 * ngb)), w_att_o, w_rec_o, w_out, h, g.reshape(1, d))


def _ffn_up_kernel(x_ref, w1_ref, w3_ref, o_ref, w1_bf, w3_bf):
    @pl.when(pl.program_id(1) == 0)
    def _():
        w1_bf[...] = w1_ref[0].astype(BF16)
        w3_bf[...] = w3_ref[0].astype(BF16)

    x = x_ref[...]
    a = jnp.dot(x, w1_bf[...], preferred_element_type=F32)
    b = jnp.dot(x, w3_bf[...], preferred_element_type=F32)
    o_ref[...] = ((a * _sigmoid(a)) * b).astype(o_ref.dtype)


def ffn_up(xn, w1, w3, layer, tm=1024, tn=512):
    n, d = xn.shape
    f = w1.shape[2]
    w_spec = lambda: pl.BlockSpec((1, d, tn), lambda j, i: (layer, 0, j))
    return pl.pallas_call(
        _ffn_up_kernel,
        grid=(f // tn, n // tm),
        in_specs=[pl.BlockSpec((tm, d), lambda j, i: (i, 0)), w_spec(), w_spec()],
        out_specs=pl.BlockSpec((tm, tn), lambda j, i: (i, j)),
        out_shape=jax.ShapeDtypeStruct((n, f), BF16),
        scratch_shapes=[pltpu.VMEM((d, tn), BF16), pltpu.VMEM((d, tn), BF16)],
        compiler_params=_cparams("arbitrary", "arbitrary"),
        name="ffn_up",
    )(xn, w1, w3)


ROUTE_E0, ROUTE_E1, ROUTE_G0, ROUTE_G1, ROUTE_R0, ROUTE_R1 = range(6)


def _router_kernel(x_ref, w_ref, tri_ref, route_ref, cnt_ref, carry):
    @pl.when(pl.program_id(0) == 0)
    def _():
        carry[...] = jnp.zeros_like(carry)

    tm = x_ref.shape[0]
    logits = jnp.dot(x_ref[...], w_ref[...], preferred_element_type=F32)
    lane = lax.broadcasted_iota(jnp.int32, (tm, LANES), 1).astype(F32)
    neg = np.float32(-np.inf)
    lg = jnp.where(lane < N_EXPERTS, logits, neg)
    m0 = jnp.max(lg, axis=-1, keepdims=True)
    e0 = jnp.min(jnp.where(lg == m0, lane, np.float32(LANES)), axis=-1, keepdims=True)
    lg1 = jnp.where(lane == e0, neg, lg)
    m1 = jnp.max(lg1, axis=-1, keepdims=True)
    e1 = jnp.min(jnp.where(lg1 == m1, lane, np.float32(LANES)), axis=-1, keepdims=True)
    ex = jnp.exp(m1 - m0)
    g0 = 1.0 / (1.0 + ex)
    g1 = ex / (1.0 + ex)
    sel0 = lane == e0
    sel1 = lane == e1
    member = jnp.where(sel0 | sel1, np.float32(1.0), np.float32(0.0))
    incl = jnp.dot(tri_ref[...], member.astype(BF16), preferred_element_type=F32)
    before = carry[...] + incl - member
    r0 = jnp.sum(jnp.where(sel0, before, 0.0), axis=-1, keepdims=True)
    r1 = jnp.sum(jnp.where(sel1, before, 0.0), axis=-1, keepdims=True)
    total = carry[...] + incl[tm - 1:tm, :]
    carry[...] = total
    cnt_ref[...] = jnp.broadcast_to(total, cnt_ref.shape)
    out = jnp.zeros((tm, LANES), F32)
    for k, v in ((ROUTE_E0, e0), (ROUTE_E1, e1), (ROUTE_G0, g0), (ROUTE_G1, g1), (ROUTE_R0, r0), (ROUTE_R1, r1)):
        out = jnp.where(lane == k, v, out)
    route_ref[...] = out


def router(hn, w_router, tm=512):
    n, d = hn.shape
    wr = jnp.zeros((d, LANES), BF16).at[:, :N_EXPERTS].set(w_router.astype(BF16))
    tri = jnp.asarray(np.tril(np.ones((tm, tm), np.float32)), BF16)
    return pl.pallas_call(
        _router_kernel,
        grid=(n // tm,),
        in_specs=[pl.BlockSpec((tm, d), lambda i: (i, 0)),
                  pl.BlockSpec((d, LANES), lambda i: (0, 0)),
                  pl.BlockSpec((tm, tm), lambda i: (0, 0))],
        out_specs=[pl.BlockSpec((tm, LANES), lambda i: (i, 0)),
                   pl.BlockSpec((SUBLANES, LANES), lambda i: (0, 0))],
        out_shape=[jax.ShapeDtypeStruct((n, LANES), F32),
                   jax.ShapeDtypeStruct((SUBLANES, LANES), F32)],
        scratch_shapes=[pltpu.VMEM((1, LANES), F32)],
        compiler_params=_cparams("arbitrary"),
        name="router",
    )(hn, wr, tri)


def _dispatch_kernel(dest_ref, zero_ref, h_ref, g_ref, xs_ref, zbuf, nbuf, sems, zsem, *, tm):
    step = pl.program_id(0)
    n_steps = pl.num_programs(0)
    slot = lax.rem(step, 2)
    eb = EXPERT_BLOCK

    @pl.when(step == 0)
    def _():
        zbuf[...] = jnp.zeros_like(zbuf)
        for z in range(zero_ref.shape[0]):
            @pl.when(zero_ref[z] >= 0)
            def _():
                cp = pltpu.make_async_copy(zbuf, xs_ref.at[pl.ds(pl.multiple_of(zero_ref[z], eb), eb)], zsem)
                cp.start()
                cp.wait()

    base = step * (tm * TOP_K)
    nbuf[slot] = _rms(h_ref[...]) * g_ref[...]

    def row_copy(s, r, d):
        return pltpu.make_async_copy(nbuf.at[s, pl.ds(r, 1)], xs_ref.at[pl.ds(d, 1)], sems.at[s])

    def issue(r, carry):
        for k in range(TOP_K):
            row_copy(slot, r, dest_ref[base + TOP_K * r + k]).start(priority=k % 2)
        return carry

    def wait_all(s):
        for _ in range(TOP_K * tm):
            row_copy(s, 0, 0).wait()

    lax.fori_loop(0, tm, issue, 0, unroll=ROW_DMA_UNROLL)

    @pl.when(step >= 1)
    def _():
        wait_all(1 - slot)

    @pl.when(step == n_steps - 1)
    def _():
        wait_all(slot)


def dispatch(h, g, dest, zero_start, n_rows, tm=256):
    n, d = h.shape
    return pl.pallas_call(
        functools.partial(_dispatch_kernel, tm=tm),
        grid_spec=pltpu.PrefetchScalarGridSpec(
            num_scalar_prefetch=2,
            grid=(n // tm,),
            in_specs=[pl.BlockSpec((tm, d), lambda i, dest, tail: (i, 0)),
                      pl.BlockSpec((1, d), lambda i, dest, tail: (0, 0))],
            out_specs=pl.BlockSpec(memory_space=pl.ANY),
            scratch_shapes=[pltpu.VMEM((EXPERT_BLOCK, d), F32),
                            pltpu.VMEM((2, tm, d), F32),
                            pltpu.SemaphoreType.DMA((2,)),
                            pltpu.SemaphoreType.DMA]),
        out_shape=jax.ShapeDtypeStruct((n_rows, d), F32),
        compiler_params=_cparams("arbitrary"),
        name="moe_dispatch",
    )(dest, zero_start, h, g.reshape(1, d))


def _grouped_rows_kernel(start_ref, count_ref, spare_ref, x_hbm, *refs, n_weights, tm, tn, tile_fn):
    w_refs = refs[:n_weights]
    o_hbm = refs[n_weights]
    w_bf = refs[n_weights + 1:2 * n_weights + 1]
    xbuf, obuf, zbuf, xsem, osem, zsem = refs[2 * n_weights + 1:]
    j = pl.program_id(0)
    e = pl.program_id(1)
    nb = count_ref[e]
    first = start_ref[e]
    col0 = pl.multiple_of(j * tn, tn)

    n_in = xbuf.shape[0]
    n_out = obuf.shape[0]
    ub = xbuf.shape[1] // tm
    n_units = lax.div(nb + (ub - 1), ub)
    n_piece = -(-(tm * xbuf.shape[2] * xbuf.dtype.itemsize) // MAX_PLAIN_DMA_BYTES)
    piece = tm // n_piece

    def x_copies(u, b):
        slot = lax.rem(u, n_in)
        r0 = (first + u * ub + b) * tm
        return [pltpu.make_async_copy(x_hbm.at[pl.ds(pl.multiple_of(r0 + p * piece, piece), piece)],
                                      xbuf.at[slot, pl.ds(b * tm + p * piece, piece)], xsem.at[slot])
                for p in range(n_piece)]

    def o_copy(u, b):
        slot = lax.rem(u, n_out)
        rows = pl.ds(pl.multiple_of((first + u * ub + b) * tm, tm), tm)
        return pltpu.make_async_copy(obuf.at[slot, pl.ds(b * tm, tm)], o_hbm.at[rows, pl.ds(col0, tn)],
                                     osem.at[slot])

    def for_blocks(u, fn):
        fn(0)
        for b in range(1, ub):
            @pl.when(u * ub + b < nb)
            def _():
                fn(b)

    def x_start(u):
        for_blocks(u, lambda b: [cp.start(priority=1) for cp in x_copies(u, b)])

    def x_wait(u):
        for_blocks(u, lambda b: [cp.wait() for cp in x_copies(u, b)])

    @pl.when(nb > 0)
    def _():
        for ahead in range(n_in - 1):
            @pl.when(ahead < n_units)
            def _():
                x_start(ahead)
        for wb, w in zip(w_bf, w_refs):
            wb[...] = w[0, 0].astype(BF16)

        def unit(u, carry):
            x_wait(u)

            @pl.when(u + (n_in - 1) < n_units)
            def _():
                x_start(u + (n_in - 1))

            @pl.when(u >= n_out)
            def _():
                for_blocks(u - n_out, lambda b: o_copy(u - n_out, b).wait())

            slot_in = lax.rem(u, n_in)
            slot_out = lax.rem(u, n_out)
            have = nb - u * ub

            @pl.when(have >= ub)
            def _():
                obuf[slot_out] = tile_fn(xbuf[slot_in], *w_bf)

            for short in range(1, ub):
                @pl.when(have == short)
                def _():
                    obuf[slot_out, :short * tm] = tile_fn(xbuf[slot_in, :short * tm], *w_bf)

            for_blocks(u, lambda b: o_copy(u, b).start())
            return carry

        lax.fori_loop(0, n_units, unit, 0)

        for back in range(n_out, 0, -1):
            @pl.when(n_units >= back)
            def _():
                for_blocks(n_units - back, lambda b: o_copy(n_units - back, b).wait())

    @pl.when(e == pl.num_programs(1) - 1)
    def _():
        zbuf[...] = jnp.zeros_like(zbuf)

        def z_copy(z):
            rows = pl.ds(pl.multiple_of(spare_ref[z], tm), tm)
            return pltpu.make_async_copy(zbuf, o_hbm.at[rows, pl.ds(col0, tn)], zsem)

        for z in range(spare_ref.shape[0]):
            @pl.when(spare_ref[z] >= 0)
            def _():
                z_copy(z).start()
        for z in range(spare_ref.shape[0]):
            @pl.when(spare_ref[z] >= 0)
            def _():
                z_copy(z).wait()


def _grouped_rows(x, weights, layer, groups, out_dtype, tn, tile_fn, name):
    start_blk, count_blk, spare_start = groups
    rows, k = x.shape
    n_exp = weights[0].shape[1]
    width = weights[0].shape[3]
    tm = EXPERT_BLOCK
    w_spec = lambda: pl.BlockSpec((1, 1, k, tn), lambda j, e, *_: (layer, e, 0, j))
    return pl.pallas_call(
        functools.partial(_grouped_rows_kernel, n_weights=len(weights), tm=tm, tn=tn, tile_fn=tile_fn),
        grid_spec=pltpu.PrefetchScalarGridSpec(
            num_scalar_prefetch=3,
            grid=(width // tn, n_exp),
            in_specs=[pl.BlockSpec(memory_space=pl.ANY)] + [w_spec() for _ in weights],
            out_specs=pl.BlockSpec(memory_space=pl.ANY),
            scratch_shapes=[pltpu.VMEM((k, tn), BF16) for _ in weights] + [
                pltpu.VMEM((GROUPED_IN_SLOTS, GROUPED_UNIT_BLOCKS * tm, k), x.dtype),
                pltpu.VMEM((GROUPED_OUT_SLOTS, GROUPED_UNIT_BLOCKS * tm, tn), out_dtype),
                pltpu.VMEM((tm, tn), out_dtype),
                pltpu.SemaphoreType.DMA((GROUPED_IN_SLOTS,)),
                pltpu.SemaphoreType.DMA((GROUPED_OUT_SLOTS,)),
                pltpu.SemaphoreType.DMA]),
        out_shape=jax.ShapeDtypeStruct((rows, width), out_dtype),
        compiler_params=_cparams("arbitrary", "arbitrary"),
        name=name,
    )(start_blk, count_blk, spare_start, x, *weights)


def _swiglu_tile(x, w1_bf, w3_bf):
    xb = x.astype(BF16)
    a = jnp.dot(xb, w1_bf[...], preferred_element_type=F32)
    b = jnp.dot(xb, w3_bf[...], preferred_element_type=F32)
    return ((a * _sigmoid(a)) * b).astype(BF16)


def _matmul_tile(x, w_bf):
    return jnp.dot(x, w_bf[...], preferred_element_type=F32)


def moe_up(xs, w1, w3, layer, groups, tn=1024):
    return _grouped_rows(xs, (w1, w3), layer, groups, BF16, tn, _swiglu_tile, "moe_up")


def moe_down(act, w2, layer, groups, tn=512):
    return _grouped_rows(act, (w2,), layer, groups, F32, tn, _matmul_tile, "moe_down")


def _combine_kernel(dest_ref, ys_ref, route_ref, h_ref, g_ref, hout_ref, n_ref, bufs, sems, *, tm):
    step = pl.program_id(0)
    n_steps = pl.num_programs(0)
    slot = lax.rem(step, 2)

    def row_copy(k, s, r, d):
        return pltpu.make_async_copy(ys_ref.at[pl.ds(d, 1)], bufs.at[k, s, pl.ds(r, 1)], sems.at[k, s])

    def gather(st, s):
        base = st * (tm * TOP_K)

        def issue(r, carry):
            for k in range(TOP_K):
                row_copy(k, s, r, dest_ref[base + TOP_K * r + k]).start(priority=k % 2)
            return carry

        lax.fori_loop(0, tm, issue, 0, unroll=ROW_DMA_UNROLL)

    @pl.when(step == 0)
    def _():
        gather(0, 0)

    @pl.when(step + 1 < n_steps)
    def _():
        gather(step + 1, 1 - slot)

    for k in range(TOP_K):
        for _ in range(tm):
            row_copy(k, slot, 0, 0).wait()

    route = route_ref[...]
    g0 = route[:, ROUTE_G0:ROUTE_G0 + 1]
    g1 = route[:, ROUTE_G1:ROUTE_G1 + 1]
    hn = h_ref[...] + (g0 * bufs[0, slot] + g1 * bufs[1, slot])
    hout_ref[...] = hn
    n_ref[...] = (_rms(hn) * g_ref[...]).astype(n_ref.dtype)


def combine(ys, dest, route, h, g, norm_dtype, tm=256):
    n, d = h.shape
    row = lambda: pl.BlockSpec((tm, d), lambda i, dest: (i, 0))
    return pl.pallas_call(
        functools.partial(_combine_kernel, tm=tm),
        grid_spec=pltpu.PrefetchScalarGridSpec(
            num_scalar_prefetch=1,
            grid=(n // tm,),
            in_specs=[pl.BlockSpec(memory_space=pl.ANY),
                      pl.BlockSpec((tm, LANES), lambda i, dest: (i, 0)),
                      row(),
                      pl.BlockSpec((1, d), lambda i, dest: (0, 0))],
            out_specs=[row(), row()],
            scratch_shapes=[pltpu.VMEM((TOP_K, 2, tm, d), F32),
                            pltpu.SemaphoreType.DMA((TOP_K, 2))]),
        out_shape=[jax.ShapeDtypeStruct((n, d), F32), jax.ShapeDtypeStruct((n, d), norm_dtype)],
        compiler_params=_cparams("arbitrary"),
        name="moe_combine",
    )(dest, ys, route, h, g.reshape(1, d))


def moe_layer(hn_bf, h, g_ffn, w_router, w1, w3, w2, layer, g_next, norm_dtype):
    n = h.shape[0]
    eb = EXPERT_BLOCK
    route, cnt = router(hn_bf, w_router)
    counts = cnt[0, :N_EXPERTS].astype(jnp.int32)
    padded = (counts + eb - 1) // eb * eb
    pend = jnp.cumsum(padded)
    pstart = pend - padded
    experts = route[:, ROUTE_E0:ROUTE_E1 + 1].astype(jnp.int32)
    ranks = route[:, ROUTE_R0:ROUTE_R1 + 1].astype(jnp.int32)
    dest = (pstart[experts] + ranks).reshape(-1)
    n_blocks = -(-(n * TOP_K) // eb) + N_EXPERTS
    tail_start = jnp.where(padded > 0, pend - eb, -1)
    spare = pend[-1] + jnp.arange(N_EXPERTS) * eb
    spare_start = jnp.where(spare < n_blocks * eb, spare, -1).astype(jnp.int32)
    zero_start = jnp.concatenate([tail_start.astype(jnp.int32), spare_start])
    groups = ((pstart // eb).astype(jnp.int32), (padded // eb).astype(jnp.int32), spare_start)
    xs = dispatch(h, g_ffn, dest, zero_start, n_blocks * eb)
    act = moe_up(xs, w1, w3, layer, groups)
    ys = moe_down(act, w2, layer, groups)
    return combine(ys, dest, route, h, g_next, norm_dtype)


def kernel(x, norm_mix_g, norm_ffn_g, norm_final_g, w_in, rel_bias, conv_w, conv_b, lru_wa, lru_ba, lru_wx, lru_bx, lru_lambda, w_att_o, w_rec_o, w_out, ffn_w1, ffn_w3, ffn_w2, router_w, moe_w1, moe_w3, moe_w2):
    b, s, d = x.shape
    n = b * s
    depth = w_in.shape[0]
    aw = N_HEADS * HEAD_DIM
    lw = conv_w.shape[2]
    assert lw == aw and d % aw == 0, "column-block indexing assumes equal branch widths"
    h = x.reshape(n, d)
    xn = rmsnorm(h, norm_mix_g[0], BF16)
    out = None
    for l in range(depth):
        cols = in_projection(xn, w_in, l, aw, HEAD_DIM ** -0.5 * LOG2_E)
        att = chunked_attention(cols, _rel_bias_table(rel_bias[l]), s)
        w_gates = jnp.concatenate([lru_wa[l], lru_wx[l]], axis=-1).astype(BF16)
        rec = recurrent_branch(cols, conv_w[l], conv_b[l], w_gates, lru_ba[l], lru_bx[l], lru_lambda[l],
                               b, s, rx_col=3)
        last = l == depth - 1
        g_next = norm_final_g if last else norm_mix_g[l + 1]
        next_dtype = F32 if last else BF16
        i = l // 2
        dense = l % 2 == 0
        h, hn = mix_out(att, rec, cols, 5, w_att_o[l].astype(BF16), w_rec_o[l].astype(BF16),
                        w_out[l].astype(BF16), h, norm_ffn_g[l], [BF16])
        if dense:
            act = ffn_up(hn, ffn_w1, ffn_w3, i)
            h, xn = proj_res_norm(act, ffn_w2[i].astype(BF16), h, g_next, [next_dtype])
        else:
            h, xn = moe_layer(hn, h, norm_ffn_g[l], router_w[i], moe_w1, moe_w3, moe_w2, i, g_next, next_dtype)
        out = xn
    return out.reshape(b, s, d)
```
